```python
import math
import jax, jax.numpy as jnp
from jax import lax
import numpy as np

D_MODEL = 1024
BATCH = 32
SEQ = 2048
DEPTH = 1

CHUNK = 64
Q_BLOCK = 128
HEAD_DIM = 64
MIX_WIDTH = D_MODEL
WIDTH_A = MIX_WIDTH // 2
WIDTH_B = MIX_WIDTH - WIDTH_A
N_HEADS_A = WIDTH_A // HEAD_DIM
N_HEADS_B = WIDTH_B // HEAD_DIM
DIFF_DIM = HEAD_DIM // 2
LEFT_CHUNKS = 8
BAND = (LEFT_CHUNKS + 1) * CHUNK
MAX_REL_DIST = 256
D_FF = 4 * D_MODEL
ROPE_THETA = 10000.0
EPS = 1e-6
QA_COLS = N_HEADS_A * 2 * DIFF_DIM
KA_COLS = N_HEADS_A * 2 * DIFF_DIM
VA_COLS = N_HEADS_A * HEAD_DIM
QB_COLS = N_HEADS_B * HEAD_DIM
KB_COLS = N_HEADS_B * HEAD_DIM
VB_COLS = N_HEADS_B * HEAD_DIM
IN_COLS = QA_COLS + KA_COLS + VA_COLS + QB_COLS + KB_COLS + VB_COLS

kernel_name = "hybrid_diffattn_chunkrel_block"


def rmsnorm(x, g):
    xf = x.astype(jnp.float32)
    y = xf * lax.rsqrt(jnp.mean(xf * xf, axis=-1, keepdims=True) + EPS)
    return y.astype(x.dtype) * g


def rope(x, pos):
    d = x.shape[-1]
    half = d // 2
    inv_freq = ROPE_THETA ** (-jnp.arange(half, dtype=jnp.float32) / half)
    ang = pos.astype(jnp.float32)[:, None] * inv_freq[None, :]
    cos = jnp.cos(ang)[:, None, :]
    sin = jnp.sin(ang)[:, None, :]
    xf = x.astype(jnp.float32)
    x1, x2 = xf[..., :half], xf[..., half:]
    out = jnp.concatenate([x1 * cos - x2 * sin, x2 * cos + x1 * sin], axis=-1)
    return out.astype(x.dtype)


def diff_attention(q, k, v, lam, subln_g, lam_init):
    S = q.shape[1]
    scale = DIFF_DIM ** -0.5
    chunk_id = jnp.arange(S) // CHUNK
    outs = []
    for i in range(S // Q_BLOCK):
        q0, q1 = i * Q_BLOCK, (i + 1) * Q_BLOCK
        qb, kb, vb = q[:, q0:q1], k[:, :q1], v[:, :q1]
        s = jnp.einsum('bqhcd,bkhcd->bhcqk', qb, kb).astype(jnp.float32) * scale
        mask = chunk_id[None, :q1] <= chunk_id[q0:q1, None]
        s = jnp.where(mask, s, -jnp.inf)
        p = jax.nn.softmax(s, axis=-1)
        a = p[:, :, 0] - lam * p[:, :, 1]
        outs.append(jnp.einsum('bhqk,bkhd->bqhd', a.astype(vb.dtype), vb))
    o = jnp.concatenate(outs, axis=1)
    return rmsnorm(o, subln_g) * (1.0 - lam_init)


def chunked_rel_attention(q, k, v, rel_table):
    B, S, H, D = q.shape
    nc = S // CHUNK
    pad = LEFT_CHUNKS * CHUNK
    scale = D ** -0.5
    kp = jnp.pad(k, ((0, 0), (pad, 0), (0, 0), (0, 0)))
    vp = jnp.pad(v, ((0, 0), (pad, 0), (0, 0), (0, 0)))
    q_local = pad + jnp.arange(CHUNK)
    k_local = jnp.arange(BAND)
    dist = jnp.clip(q_local[:, None] - k_local[None, :], -MAX_REL_DIST, MAX_REL_DIST) + MAX_REL_DIST
    bias = rel_table[:, dist].astype(jnp.float32)
    qc = q.reshape(B, nc, CHUNK, H, D).transpose(1, 0, 2, 3, 4)

    def one_chunk(args):
        c, qi = args
        start = c * CHUNK
        kb = lax.dynamic_slice_in_dim(kp, start, BAND, axis=1)
        vb = lax.dynamic_slice_in_dim(vp, start, BAND, axis=1)
        s = jnp.einsum('bqhd,bkhd->bhqk', qi, kb).astype(jnp.float32) * scale + bias
        valid = (start + k_local) >= pad
        s = jnp.where(valid[None, None, None, :], s, -jnp.inf)
        p = jax.nn.softmax(s, axis=-1)
        return jnp.einsum('bhqk,bkhd->bqhd', p.astype(vb.dtype), vb)

    o = lax.map(one_chunk, (jnp.arange(nc), qc))
    return o.transpose(1, 0, 2, 3, 4).reshape(B, S, H, D)


def setup_inputs(seed: int = 0) -> dict:
    key = jax.random.key(seed)
    ks = jax.random.split(key, 16)
    f32 = jnp.float32
    n = lambda k, shape, s: jax.random.normal(k, shape, f32) * s
    return {
        "x": n(ks[0], (BATCH, SEQ, D_MODEL), 1.0),
        "w_in": n(ks[1], (DEPTH, D_MODEL, IN_COLS), D_MODEL ** -0.5),
        "w_out": n(ks[2], (DEPTH, MIX_WIDTH, D_MODEL), MIX_WIDTH ** -0.5),
        "norm1_g": 1.0 + n(ks[3], (DEPTH, D_MODEL), 0.02),
        "norm2_g": 1.0 + n(ks[4], (DEPTH, D_MODEL), 0.02),
        "final_g": 1.0 + n(ks[5], (D_MODEL,), 0.02),
        "subln_g": 1.0 + n(ks[6], (DEPTH, HEAD_DIM), 0.02),
        "lambda_q1": n(ks[7], (DEPTH, DIFF_DIM), 0.1),
        "lambda_k1": n(ks[8], (DEPTH, DIFF_DIM), 0.1),
        "lambda_q2": n(ks[9], (DEPTH, DIFF_DIM), 0.1),
        "lambda_k2": n(ks[10], (DEPTH, DIFF_DIM), 0.1),
        "rel_bias": n(ks[11], (DEPTH, N_HEADS_B, 2 * MAX_REL_DIST + 1), 0.1),
        "w_ff1": n(ks[12], (DEPTH, D_MODEL, D_FF), D_MODEL ** -0.5),
        "w_ff2": n(ks[13], (DEPTH, D_FF, D_MODEL), D_FF ** -0.5),
    }


def reference(x, w_in, w_out, norm1_g, norm2_g, final_g, subln_g, lambda_q1, lambda_k1,
              lambda_q2, lambda_k2, rel_bias, w_ff1, w_ff2):
    B, S, _ = x.shape
    pos = jnp.arange(S)
    splits = np.cumsum([QA_COLS, KA_COLS, VA_COLS, QB_COLS, KB_COLS]).tolist()
    for l in range(DEPTH):
        lam_init = 0.8 - 0.6 * math.exp(-0.3 * l)
        h = rmsnorm(x, norm1_g[l])
        proj = h @ w_in[l]
        qa, ka, va, qb, kb, vb = jnp.split(proj, splits, axis=-1)
        qa = rope(qa.reshape(B, S, N_HEADS_A * 2, DIFF_DIM), pos).reshape(B, S, N_HEADS_A, 2, DIFF_DIM)
        ka = rope(ka.reshape(B, S, N_HEADS_A * 2, DIFF_DIM), pos).reshape(B, S, N_HEADS_A, 2, DIFF_DIM)
        va = va.reshape(B, S, N_HEADS_A, HEAD_DIM)
        lam = (jnp.exp(jnp.sum(lambda_q1[l].astype(jnp.float32) * lambda_k1[l].astype(jnp.float32)))
               - jnp.exp(jnp.sum(lambda_q2[l].astype(jnp.float32) * lambda_k2[l].astype(jnp.float32)))
               + lam_init)
        oa = diff_attention(qa, ka, va, lam, subln_g[l], lam_init)
        qb = qb.reshape(B, S, N_HEADS_B, HEAD_DIM)
        kb = kb.reshape(B, S, N_HEADS_B, HEAD_DIM)
        vb = vb.reshape(B, S, N_HEADS_B, HEAD_DIM)
        ob = chunked_rel_attention(qb, kb, vb, rel_bias[l])
        mix = jnp.concatenate([oa.reshape(B, S, WIDTH_A), ob.reshape(B, S, WIDTH_B)], axis=-1)
        x = x + mix @ w_out[l]
        h = rmsnorm(x, norm2_g[l])
        x = x + jnp.square(jax.nn.relu(h @ w_ff1[l])) @ w_ff2[l]
    return rmsnorm(x, final_g)
```

```python
import functools
import math

import jax
import jax.numpy as jnp
import numpy as np
from jax import lax
from jax.experimental import pallas as pl
from jax.experimental.pallas import tpu as pltpu

D_MODEL = 1024
HEAD_DIM = 64
DIFF_DIM = 32
N_HEADS_A = 8
N_HEADS_B = 8
WIDTH_A = N_HEADS_A * HEAD_DIM
WIDTH_B = N_HEADS_B * HEAD_DIM
CHUNK = 64
LEFT_CHUNKS = 8
MAX_REL_DIST = 256
D_FF = 4 * D_MODEL
ROPE_THETA = 10000.0
EPS = 1e-6
LAM_INIT = 0.8 - 0.6 * math.exp(-0.3 * 0)

LANES = 128
PAIR = 2 * HEAD_DIM
N_PAIRS = WIDTH_A // PAIR
VMEM_LIMIT = 56 * 1024 * 1024

BF16 = jnp.bfloat16
F32 = jnp.float32
NEG = -1e30


def _dot(a, b):
    return jnp.dot(a, b, preferred_element_type=F32)


def _dot_nt(a, b):
    return lax.dot_general(a, b, (((1,), (1,)), ((), ())), preferred_element_type=F32)


def _proj_kernel(x_ref, g_ref, wqv_ref, wk_ref, cqT_ref, sqT_ref, ck_ref, sk_ref,
                 qa_ref, va_ref, qb_ref, vb_ref, ka_ref, kb_ref, *, tm):
    x = x_ref[0]
    ms = jnp.mean(x * x, axis=-1, keepdims=True)
    h = ((x * lax.rsqrt(ms + EPS)) * g_ref[...]).astype(BF16)
    nblk = tm // LANES

    qvT = _dot_nt(wqv_ref[...], h)
    cq = cqT_ref[...]
    sq = sqT_ref[...]
    for g in range(N_PAIRS):
        x1 = qvT[g * PAIR:g * PAIR + HEAD_DIM]
        x2 = qvT[g * PAIR + HEAD_DIM:(g + 1) * PAIR]
        r1 = (x1 * cq - x2 * sq).astype(BF16)
        r2 = (x2 * cq + x1 * sq).astype(BF16)
        for t in range(nblk):
            qa_ref[0, t, g * PAIR:g * PAIR + HEAD_DIM, :] = r1[:, t * LANES:(t + 1) * LANES]
            qa_ref[0, t, g * PAIR + HEAD_DIM:(g + 1) * PAIR, :] = r2[:, t * LANES:(t + 1) * LANES]
    va = qvT[WIDTH_A:2 * WIDTH_A].astype(BF16)
    qb = (qvT[2 * WIDTH_A:2 * WIDTH_A + WIDTH_B] * (HEAD_DIM ** -0.5)).astype(BF16)
    vb = qvT[2 * WIDTH_A + WIDTH_B:].astype(BF16)
    for t in range(nblk):
        va_ref[0, t] = va[:, t * LANES:(t + 1) * LANES]
        qb_ref[0, t] = qb[:, t * LANES:(t + 1) * LANES]
        vb_ref[0, t] = vb[:, t * LANES:(t + 1) * LANES]

    k = _dot(h, wk_ref[...])
    ck = ck_ref[...]
    sk = sk_ref[...]
    for g in range(N_PAIRS):
        xk = k[:, g * PAIR:(g + 1) * PAIR]
        ka_ref[0, :, g * PAIR:(g + 1) * PAIR] = (xk * ck + pltpu.roll(xk, HEAD_DIM, 1) * sk).astype(BF16)
    kb_ref[0] = k[:, WIDTH_A:].astype(BF16)


def _project(x, g1, wqv, wk, cqT, sqT, ck, sk, *, tm):
    B, S, _ = x.shape
    nblk = tm // LANES
    feat = jax.ShapeDtypeStruct((B, S // LANES, WIDTH_A, LANES), BF16)
    posm = jax.ShapeDtypeStruct((B, S, WIDTH_A), BF16)
    feat_spec = pl.BlockSpec((1, nblk, WIDTH_A, LANES), lambda b, i: (b, i, 0, 0))
    posm_spec = pl.BlockSpec((1, tm, WIDTH_A), lambda b, i: (b, i, 0))
    const = lambda shape: pl.BlockSpec(shape, lambda b, i: (0,) * len(shape), pipeline_mode=pl.Buffered(1))
    return pl.pallas_call(
        functools.partial(_proj_kernel, tm=tm),
        grid=(B, S // tm),
        in_specs=[
            pl.BlockSpec((1, tm, D_MODEL), lambda b, i: (b, i, 0)),
            const((1, D_MODEL)),
            const(wqv.shape),
            const(wk.shape),
            pl.BlockSpec((HEAD_DIM, tm), lambda b, i: (0, i)),
            pl.BlockSpec((HEAD_DIM, tm), lambda b, i: (0, i)),
            pl.BlockSpec((tm, PAIR), lambda b, i: (i, 0)),
            pl.BlockSpec((tm, PAIR), lambda b, i: (i, 0)),
        ],
        out_specs=[feat_spec, feat_spec, feat_spec, feat_spec, posm_spec, posm_spec],
        out_shape=[feat, feat, feat, feat, posm, posm],
        compiler_params=pltpu.CompilerParams(
            dimension_semantics=("parallel", "parallel"), vmem_limit_bytes=VMEM_LIMIT),
        name="proj_rope",
    )(x, g1, wqv, wk, cqT, sqT, ck, sk)


def _diff_attn_kernel(q_ref, k_ref, v_ref, lq1_ref, lk1_ref, lq2_ref, lk2_ref, g_ref, o_ref, *, qb, nq):
    lam = (jnp.exp(jnp.sum(lq1_ref[...] * lk1_ref[...], axis=-1, keepdims=True))
           - jnp.exp(jnp.sum(lq2_ref[...] * lk2_ref[...], axis=-1, keepdims=True)) + LAM_INIT)
    gain = g_ref[...] * (1.0 - LAM_INIT)
    sub = qb // LANES
    row = lax.broadcasted_iota(jnp.int32, (PAIR, qb), 0)
    sub_head = (row % HEAD_DIM) // (DIFF_DIM // 2)
    kk = lax.broadcasted_iota(jnp.int32, (qb, qb), 0) // CHUNK
    qq = lax.broadcasted_iota(jnp.int32, (qb, qb), 1) // CHUNK
    diag_ok = jnp.concatenate([kk <= qq] * 4, axis=1)

    def q_block(i, carry):
        qT = jnp.concatenate([q_ref[0, i * sub + t] for t in range(sub)], axis=1)
        zero = jnp.zeros_like(qT)
        w = jnp.concatenate([jnp.where(sub_head == j, qT, zero) for j in range(4)], axis=1)

        def scores(j):
            start = pl.multiple_of(j * qb, qb)
            return _dot(k_ref[0, pl.ds(start, qb), :], w)

        def pv(j, p):
            pb = p.astype(BF16)
            outs = []
            for hh in range(2):
                acc = None
                for t in range(sub):
                    vT = v_ref[0, j * sub + t, hh * HEAD_DIM:(hh + 1) * HEAD_DIM, :]
                    d = _dot(vT, pb[t * LANES:(t + 1) * LANES, 2 * qb * hh:2 * qb * (hh + 1)])
                    acc = d if acc is None else acc + d
                outs.append(acc)
            return jnp.concatenate(outs, axis=1)

        s = jnp.where(diag_ok, scores(i), NEG)
        m = jnp.max(s, axis=0, keepdims=True)
        p = jnp.exp(s - m)
        l = jnp.sum(p, axis=0, keepdims=True)
        acc = pv(i, p)

        def kv_block(j, st):
            m, l, acc = st
            s = scores(j)
            m_new = jnp.maximum(m, jnp.max(s, axis=0, keepdims=True))
            alpha = jnp.exp(m - m_new)
            p = jnp.exp(s - m_new)
            l = alpha * l + jnp.sum(p, axis=0, keepdims=True)
            acc = alpha * acc + pv(j, p)
            return m_new, l, acc

        m, l, acc = lax.fori_loop(0, i, kv_block, (m, l, acc))
        o = acc / l
        ys = []
        for hh in range(2):
            o0 = o[:, 2 * qb * hh:2 * qb * hh + qb]
            o1 = o[:, 2 * qb * hh + qb:2 * qb * (hh + 1)]
            d = o0 - lam * o1
            ms = jnp.mean(d * d, axis=0, keepdims=True)
            ys.append(d * lax.rsqrt(ms + EPS) * gain)
        y = jnp.concatenate(ys, axis=0)
        o_ref[0, pl.ds(pl.multiple_of(i * qb, qb), qb), :] = y.T.astype(BF16)
        return carry

    lax.fori_loop(0, nq, q_block, 0)


def _diff_attention(qT, k, vT, lq1, lk1, lq2, lk2, g, *, qb):
    B, S, _ = k.shape
    nblk = S // LANES
    feat_spec = pl.BlockSpec((1, nblk, PAIR, LANES), lambda b, p: (b, 0, p, 0))
    posm_spec = pl.BlockSpec((1, S, PAIR), lambda b, p: (b, 0, p))
    vec = pl.BlockSpec((1, DIFF_DIM), lambda b, p: (0, 0))
    return pl.pallas_call(
        functools.partial(_diff_attn_kernel, qb=qb, nq=S // qb),
        grid=(B, N_PAIRS),
        in_specs=[feat_spec, posm_spec, feat_spec, vec, vec, vec, vec,
                  pl.BlockSpec((HEAD_DIM, 1), lambda b, p: (0, 0))],
        out_specs=posm_spec,
        out_shape=jax.ShapeDtypeStruct((B, S, WIDTH_A), BF16),
        compiler_params=pltpu.CompilerParams(
            dimension_semantics=("parallel", "parallel"), vmem_limit_bytes=VMEM_LIMIT),
        name="diff_attn",
    )(qT, k, vT, lq1, lk1, lq2, lk2, g)


BAND_BLOCKS = LEFT_CHUNKS * CHUNK // LANES + 1


def _band_attn_kernel(q_ref, k_ref, v_ref, bias_ref, o_ref, *, nq):
    row = lax.broadcasted_iota(jnp.int32, (PAIR, LANES), 0)

    def q_block(i, first):
        nb = BAND_BLOCKS if first is None else i + 1
        lo = (i - (BAND_BLOCKS - 1)) if first is None else 0
        qT = q_ref[0, i]
        zero = jnp.zeros_like(qT)
        w = jnp.concatenate([jnp.where(row < HEAD_DIM, qT, zero),
                             jnp.where(row >= HEAD_DIM, qT, zero)], axis=1)
        if first is None:
            kblk = k_ref[0, pl.ds(pl.multiple_of(lo * LANES, LANES), nb * LANES), :]
            bias = bias_ref[0]
        else:
            kblk = k_ref[0, 0:nb * LANES, :]
            bias = bias_ref[0, (BAND_BLOCKS - nb) * LANES:, :]
        s = _dot(kblk, w) + bias
        m = jnp.max(s, axis=0, keepdims=True)
        p = jnp.exp(s - m)
        l = jnp.sum(p, axis=0, keepdims=True)
        pb = p.astype(BF16)
        outs = []
        for hh in range(2):
            acc = None
            for t in range(nb):
                vT = v_ref[0, lo + t, hh * HEAD_DIM:(hh + 1) * HEAD_DIM, :]
                d = _dot(vT, pb[t * LANES:(t + 1) * LANES, hh * LANES:(hh + 1) * LANES])
                acc = d if acc is None else acc + d
            outs.append(acc / l[:, hh * LANES:(hh + 1) * LANES])
        y = jnp.concatenate(outs, axis=0)
        if first is None:
            o_ref[0, pl.ds(pl.multiple_of(i * LANES, LANES), LANES), :] = y.T.astype(BF16)
        else:
            o_ref[0, i * LANES:(i + 1) * LANES, :] = y.T.astype(BF16)

    n_lead = min(BAND_BLOCKS - 1, nq)
    for i in range(n_lead):
        q_block(i, 0)

    def body(i, carry):
        q_block(i, None)
        return carry

    lax.fori_loop(n_lead, nq, body, 0)


def _band_attention(qT, k, vT, bias):
    B, S, _ = k.shape
    nblk = S // LANES
    feat_spec = pl.BlockSpec((1, nblk, PAIR, LANES), lambda b, p: (b, 0, p, 0))
    posm_spec = pl.BlockSpec((1, S, PAIR), lambda b, p: (b, 0, p))
    return pl.pallas_call(
        functools.partial(_band_attn_kernel, nq=nblk),
        grid=(B, N_PAIRS),
        in_specs=[feat_spec, posm_spec, feat_spec,
                  pl.BlockSpec((1, BAND_BLOCKS * LANES, 2 * LANES), lambda b, p: (p, 0, 0))],
        out_specs=posm_spec,
        out_shape=jax.ShapeDtypeStruct((B, S, WIDTH_B), BF16),
        compiler_params=pltpu.CompilerParams(
            dimension_semantics=("parallel", "parallel"), vmem_limit_bytes=VMEM_LIMIT),
        name="band_attn",
    )(qT, k, vT, bias)


def _mlp_kernel(x_ref, oa_ref, ob_ref, woa_ref, wob_ref, g2_ref, w1_ref, w2_ref, gf_ref, out_ref, *, ff_chunk):
    y = x_ref[...] + _dot(oa_ref[...], woa_ref[...]) + _dot(ob_ref[...], wob_ref[...])
    ms = jnp.mean(y * y, axis=-1, keepdims=True)
    h = ((y * lax.rsqrt(ms + EPS)) * g2_ref[...]).astype(BF16)
    mlp = None
    for c in range(D_FF // ff_chunk):
        u = _dot(h, w1_ref[:, c * ff_chunk:(c + 1) * ff_chunk])
        u = jnp.maximum(u, 0.0)
        d = _dot((u * u).astype(BF16), w2_ref[c * ff_chunk:(c + 1) * ff_chunk, :])
        mlp = d if mlp is None else mlp + d
    acc = y + mlp
    ms = jnp.mean(acc * acc, axis=-1, keepdims=True)
    out_ref[...] = (acc * lax.rsqrt(ms + EPS)) * gf_ref[...]


def _out_mlp(x2d, oa, ob, woa, wob, g2, w1, w2, gf, *, tm, ff_chunk):
    T = x2d.shape[0]
    const = lambda shape: pl.BlockSpec(shape, lambda i: (0,) * len(shape), pipeline_mode=pl.Buffered(1))
    return pl.pallas_call(
        functools.partial(_mlp_kernel, ff_chunk=ff_chunk),
        grid=(T // tm,),
        in_specs=[
            pl.BlockSpec((tm, D_MODEL), lambda i: (i, 0)),
            pl.BlockSpec((tm, WIDTH_A), lambda i: (i, 0)),
            pl.BlockSpec((tm, WIDTH_B), lambda i: (i, 0)),
            const(woa.shape), const(wob.shape), const((1, D_MODEL)),
            const(w1.shape), const(w2.shape), const((1, D_MODEL)),
        ],
        out_specs=pl.BlockSpec((tm, D_MODEL), lambda i: (i, 0)),
        out_shape=jax.ShapeDtypeStruct((T, D_MODEL), F32),
        compiler_params=pltpu.CompilerParams(
            dimension_semantics=("parallel",), vmem_limit_bytes=VMEM_LIMIT),
        name="out_mlp",
    )(x2d, oa, ob, woa, wob, g2, w1, w2, gf)


def _pair_perm():
    p = np.arange(PAIR)
    half, j, f = p // HEAD_DIM, (p % HEAD_DIM) // (DIFF_DIM // 2), p % (DIFF_DIM // 2)
    within = j * DIFF_DIM + half * (DIFF_DIM // 2) + f
    return (np.arange(N_PAIRS)[:, None] * PAIR + within[None, :]).reshape(-1)


def _rope_tables(S):
    half = DIFF_DIM // 2
    inv_freq = ROPE_THETA ** (-jnp.arange(half, dtype=F32) / half)
    ang = jnp.arange(S).astype(F32)[:, None] * inv_freq[None, :]
    cos, sin = jnp.cos(ang), jnp.sin(ang)
    scale = DIFF_DIM ** -0.5
    cqT = jnp.tile(cos.T, (HEAD_DIM // half, 1)) * scale
    sqT = jnp.tile(sin.T, (HEAD_DIM // half, 1)) * scale
    ck = jnp.tile(cos, (1, PAIR // half))
    sign = jnp.where(jnp.arange(PAIR) < HEAD_DIM, -1.0, 1.0).astype(F32)
    sk = jnp.tile(sin, (1, PAIR // half)) * sign[None, :]
    return cqT, sqT, ck, sk


def _band_bias(rel_table):
    nk = BAND_BLOCKS * LANES
    j = jnp.arange(nk)[:, None]
    i = jnp.arange(LANES)[None, :]
    dist = jnp.clip(LEFT_CHUNKS * CHUNK + i - j, -MAX_REL_DIST, MAX_REL_DIST) + MAX_REL_DIST
    qc = i // CHUNK
    kc = j // CHUNK
    ok = (kc >= qc) & (kc <= qc + LEFT_CHUNKS)
    bias = jnp.where(ok[None], rel_table[:, dist].astype(F32), NEG)
    return bias.reshape(N_PAIRS, 2, nk, LANES).transpose(0, 2, 1, 3).reshape(N_PAIRS, nk, 2 * LANES)


def kernel(x, w_in, w_out, norm1_g, norm2_g, final_g, subln_g, lambda_q1, lambda_k1, lambda_q2, lambda_k2,
           rel_bias, w_ff1, w_ff2):
    B, S, D = x.shape
    depth = w_in.shape[0]
    assert D == D_MODEL and depth == 1 and S % 256 == 0
    l = 0
    perm = _pair_perm()
    w = w_in[l]
    c0, c1, c2, c3, c4 = np.cumsum([WIDTH_A, WIDTH_A, WIDTH_A, WIDTH_B, WIDTH_B]).tolist()
    wq_a = w[:, :c0][:, perm]
    wk_a = w[:, c0:c1][:, perm]
    wqv = jnp.concatenate([wq_a, w[:, c1:c2], w[:, c2:c3], w[:, c4:]], axis=1).T.astype(BF16)
    wk = jnp.concatenate([wk_a, w[:, c3:c4]], axis=1).astype(BF16)
    cqT, sqT, ck, sk = _rope_tables(S)

    qaT, vaT, qbT, vbT, ka, kb = _project(
        x, norm1_g[l].reshape(1, D), wqv, wk, cqT, sqT, ck, sk, tm=512)

    oa = _diff_attention(
        qaT, ka, vaT,
        lambda_q1[l].reshape(1, -1).astype(F32), lambda_k1[l].reshape(1, -1).astype(F32),
        lambda_q2[l].reshape(1, -1).astype(F32), lambda_k2[l].reshape(1, -1).astype(F32),
        subln_g[l].reshape(HEAD_DIM, 1), qb=256)
    ob = _band_attention(qbT, kb, vbT, _band_bias(rel_bias[l]))

    wo = w_out[l].astype(BF16)
    out = _out_mlp(
        x.reshape(B * S, D), oa.reshape(B * S, WIDTH_A), ob.reshape(B * S, WIDTH_B),
        wo[:WIDTH_A], wo[WIDTH_A:], norm2_g[l].reshape(1, D),
        w_ff1[l].astype(BF16), w_ff2[l].astype(BF16), final_g.reshape(1, D), tm=512, ff_chunk=1024)
    return out.reshape(B, S, D)
```

```python
import functools
import math

import jax
import jax.numpy as jnp
import numpy as np
from jax import lax
from jax.experimental import pallas as pl
from jax.experimental.pallas import tpu as pltpu

D_MODEL = 1024
HEAD_DIM = 64
DIFF_DIM = 32
N_HEADS_A = 8
N_HEADS_B = 8
WIDTH_A = N_HEADS_A * HEAD_DIM
WIDTH_B = N_HEADS_B * HEAD_DIM
CHUNK = 64
LEFT_CHUNKS = 8
MAX_REL_DIST = 256
D_FF = 4 * D_MODEL
ROPE_THETA = 10000.0
EPS = 1e-6
LAM_INIT = 0.8 - 0.6 * math.exp(-0.3 * 0)

LANES = 128
PAIR = 2 * HEAD_DIM
N_PAIRS = WIDTH_A // PAIR
VMEM_LIMIT = 56 * 1024 * 1024

BF16 = jnp.bfloat16
F32 = jnp.float32
NEG = -1e30
ONES_ROWS = 16
LOG2E = math.log2(math.e)


def _dot(a, b):
    return jnp.dot(a, b, preferred_element_type=F32)


def _dot_nt(a, b):
    return lax.dot_general(a, b, (((1,), (1,)), ((), ())), preferred_element_type=F32)


def _proj_kernel(x_ref, g_ref, wqv_ref, wk_ref, cqT_ref, sqT_ref, ck_ref, sk_ref,
                 qa_ref, va_ref, qb_ref, vb_ref, ka_ref, kb_ref, *, tm):
    x = x_ref[0]
    ms = jnp.mean(x * x, axis=-1, keepdims=True)
    h = ((x * lax.rsqrt(ms + EPS)) * g_ref[...]).astype(BF16)
    nblk = tm // LANES

    qvT = _dot_nt(wqv_ref[...], h)
    cq = cqT_ref[...]
    sq = sqT_ref[...]
    for g in range(N_PAIRS):
        x1 = qvT[g * PAIR:g * PAIR + HEAD_DIM]
        x2 = qvT[g * PAIR + HEAD_DIM:(g + 1) * PAIR]
        r1 = (x1 * cq - x2 * sq).astype(BF16)
        r2 = (x2 * cq + x1 * sq).astype(BF16)
        for t in range(nblk):
            qa_ref[0, t, g * PAIR:g * PAIR + HEAD_DIM, :] = r1[:, t * LANES:(t + 1) * LANES]
            qa_ref[0, t, g * PAIR + HEAD_DIM:(g + 1) * PAIR, :] = r2[:, t * LANES:(t + 1) * LANES]
    va = qvT[WIDTH_A:2 * WIDTH_A].astype(BF16)
    qb = (qvT[2 * WIDTH_A:2 * WIDTH_A + WIDTH_B] * (HEAD_DIM ** -0.5 * LOG2E)).astype(BF16)
    vb = qvT[2 * WIDTH_A + WIDTH_B:].astype(BF16)
    for t in range(nblk):
        va_ref[0, t] = va[:, t * LANES:(t + 1) * LANES]
        qb_ref[0, t] = qb[:, t * LANES:(t + 1) * LANES]
        vb_ref[0, t] = vb[:, t * LANES:(t + 1) * LANES]

    k = _dot(h, wk_ref[...])
    ck = ck_ref[...]
    sk = sk_ref[...]
    for g in range(N_PAIRS):
        xk = k[:, g * PAIR:(g + 1) * PAIR]
        ka_ref[0, :, g * PAIR:(g + 1) * PAIR] = (xk * ck + pltpu.roll(xk, HEAD_DIM, 1) * sk).astype(BF16)
    kb_ref[0] = k[:, WIDTH_A:].astype(BF16)


def _project(x, g1, wqv, wk, cqT, sqT, ck, sk, *, tm):
    B, S, _ = x.shape
    nblk = tm // LANES
    feat = jax.ShapeDtypeStruct((B, S // LANES, WIDTH_A, LANES), BF16)
    posm = jax.ShapeDtypeStruct((B, S, WIDTH_A), BF16)
    feat_spec = pl.BlockSpec((1, nblk, WIDTH_A, LANES), lambda b, i: (b, i, 0, 0))
    posm_spec = pl.BlockSpec((1, tm, WIDTH_A), lambda b, i: (b, i, 0))
    const = lambda shape: pl.BlockSpec(shape, lambda b, i: (0,) * len(shape), pipeline_mode=pl.Buffered(1))
    return pl.pallas_call(
        functools.partial(_proj_kernel, tm=tm),
        grid=(B, S // tm),
        in_specs=[
            pl.BlockSpec((1, tm, D_MODEL), lambda b, i: (b, i, 0)),
            const((1, D_MODEL)),
            const(wqv.shape),
            const(wk.shape),
            pl.BlockSpec((HEAD_DIM, tm), lambda b, i: (0, i)),
            pl.BlockSpec((HEAD_DIM, tm), lambda b, i: (0, i)),
            pl.BlockSpec((tm, PAIR), lambda b, i: (i, 0)),
            pl.BlockSpec((tm, PAIR), lambda b, i: (i, 0)),
        ],
        out_specs=[feat_spec, feat_spec, feat_spec, feat_spec, posm_spec, posm_spec],
        out_shape=[feat, feat, feat, feat, posm, posm],
        compiler_params=pltpu.CompilerParams(
            dimension_semantics=("parallel", "parallel"), vmem_limit_bytes=VMEM_LIMIT),
        name="proj_rope",
    )(x, g1, wqv, wk, cqT, sqT, ck, sk)


def _diff_attn_kernel(q_ref, k_ref, v_ref, lq1_ref, lk1_ref, lq2_ref, lk2_ref, g_ref, o_ref, s_buf, p_buf,
                      *, qb, nq):
    lam = (jnp.exp(jnp.sum(lq1_ref[...] * lk1_ref[...], axis=-1, keepdims=True))
           - jnp.exp(jnp.sum(lq2_ref[...] * lk2_ref[...], axis=-1, keepdims=True)) + LAM_INIT)
    gain = g_ref[...] * (1.0 - LAM_INIT)
    sub = qb // LANES
    row = lax.broadcasted_iota(jnp.int32, (PAIR, qb), 0)
    sub_head = (row % HEAD_DIM) // (DIFF_DIM // 2)
    kk = lax.broadcasted_iota(jnp.int32, (qb, qb), 0) // CHUNK
    qq = lax.broadcasted_iota(jnp.int32, (qb, qb), 1) // CHUNK
    diag_mask = jnp.where(kk <= qq, 0.0, NEG)
    ones = jnp.ones((ONES_ROWS, qb), BF16)

    def q_block(i, carry):
        qT = jnp.concatenate([q_ref[0, i * sub + t] for t in range(sub)], axis=1)
        zero = jnp.zeros_like(qT)
        w = jnp.concatenate([jnp.where(sub_head == j, qT, zero) for j in range(4)], axis=1)

        def scores(blk, masked):
            kblk = k_ref[0, pl.ds(pl.multiple_of(blk * qb, qb), qb), :]
            s = _dot(kblk, w)
            if masked:
                s = s + jnp.concatenate([diag_mask] * 4, axis=1)
            s_buf[...] = s
            return jnp.max(s, axis=0, keepdims=True)

        def weights(m_blk, m):
            m_new = jnp.maximum(m, m_blk)
            p_buf[...] = jnp.exp2(s_buf[...] - m_new).astype(BF16)
            return m_new, jnp.exp2(m - m_new)

        def accum(blk, alpha, acc):
            ds = []
            for hh in range(2):
                vx = jnp.concatenate(
                    [jnp.concatenate([v_ref[0, blk * sub + t, hh * HEAD_DIM:(hh + 1) * HEAD_DIM, :]
                                      for t in range(sub)], axis=1), ones], axis=0)
                ds.append(_dot(vx, p_buf[:, 2 * qb * hh:2 * qb * (hh + 1)]))
            return alpha * acc + jnp.concatenate(ds, axis=1)

        p_buf[...] = jnp.zeros((qb, 4 * qb), BF16)
        m_blk = scores(i, True)
        m = jnp.full((1, 4 * qb), NEG, F32)
        alpha = jnp.ones((1, 4 * qb), F32)
        acc = jnp.zeros((HEAD_DIM + ONES_ROWS, 4 * qb), F32)

        def step(t, st):
            m_blk, m, alpha, acc = st
            acc = accum(jnp.where(t == 2, i, jnp.maximum(t - 3, 0)), alpha, acc)
            m, alpha = weights(m_blk, m)
            m_blk = scores(t - 1, False)
            return m_blk, m, alpha, acc

        m_blk, m, alpha, acc = lax.fori_loop(1, i + 1, step, (m_blk, m, alpha, acc))
        acc = accum(jnp.where(i == 1, i, jnp.maximum(i - 2, 0)), alpha, acc)
        m, alpha = weights(m_blk, m)
        acc = accum(jnp.maximum(i - 1, 0), alpha, acc)

        ys = []
        for hh in range(2):
            o = []
            for c in range(2):
                a = acc[:, (2 * hh + c) * qb:(2 * hh + c + 1) * qb]
                o.append(a[:HEAD_DIM] * (1.0 / a[HEAD_DIM:HEAD_DIM + 1]))
            d = o[0] - lam * o[1]
            ms = jnp.mean(d * d, axis=0, keepdims=True)
            ys.append(d * lax.rsqrt(ms + EPS) * gain)
        y = jnp.concatenate(ys, axis=0)
        o_ref[0, pl.ds(pl.multiple_of(i * qb, qb), qb), :] = y.T.astype(BF16)
        return carry

    lax.fori_loop(0, nq, q_block, 0)


def _diff_attention(qT, k, vT, lq1, lk1, lq2, lk2, g, *, qb):
    B, S, _ = k.shape
    nblk = S // LANES
    feat_spec = pl.BlockSpec((1, nblk, PAIR, LANES), lambda b, p: (b, 0, p, 0))
    posm_spec = pl.BlockSpec((1, S, PAIR), lambda b, p: (b, 0, p))
    vec = pl.BlockSpec((1, DIFF_DIM), lambda b, p: (0, 0))
    return pl.pallas_call(
        functools.partial(_diff_attn_kernel, qb=qb, nq=S // qb),
        grid=(B, N_PAIRS),
        in_specs=[feat_spec, posm_spec, feat_spec, vec, vec, vec, vec,
                  pl.BlockSpec((HEAD_DIM, 1), lambda b, p: (0, 0))],
        out_specs=posm_spec,
        out_shape=jax.ShapeDtypeStruct((B, S, WIDTH_A), BF16),
        scratch_shapes=[pltpu.VMEM((qb, 4 * qb), F32), pltpu.VMEM((qb, 4 * qb), BF16)],
        compiler_params=pltpu.CompilerParams(
            dimension_semantics=("parallel", "parallel"), vmem_limit_bytes=VMEM_LIMIT),
        name="diff_attn",
    )(qT, k, vT, lq1, lk1, lq2, lk2, g)


BAND_BLOCKS = LEFT_CHUNKS * CHUNK // LANES + 1
BAND_KEYS = BAND_BLOCKS * LANES


def _band_attn_kernel(q_ref, k_ref, v_ref, bias_ref, o_ref, s_buf, p_buf, *, nq):
    row = lax.broadcasted_iota(jnp.int32, (PAIR, LANES), 0)
    ones = jnp.ones((ONES_ROWS, BAND_KEYS), BF16)
    rows_h = HEAD_DIM + ONES_ROWS

    def first_block(i):
        return jnp.maximum(i - (BAND_BLOCKS - 1), 0)

    def scores(i):
        qT = q_ref[0, i]
        zero = jnp.zeros_like(qT)
        w = jnp.concatenate([jnp.where(row < HEAD_DIM, qT, zero),
                             jnp.where(row >= HEAD_DIM, qT, zero)], axis=1)
        kblk = k_ref[0, pl.ds(pl.multiple_of(first_block(i) * LANES, LANES), BAND_KEYS), :]
        s = _dot(kblk, w) + bias_ref[0, jnp.minimum(i, BAND_BLOCKS - 1)]
        s_buf[...] = s
        return jnp.max(s, axis=0, keepdims=True)

    def weights(m):
        p_buf[...] = jnp.exp2(s_buf[...] - m).astype(BF16)

    def output(i):
        lo = first_block(i)
        vx = jnp.concatenate(
            [part for hh in range(2) for part in (
                jnp.concatenate([v_ref[0, lo + t, hh * HEAD_DIM:(hh + 1) * HEAD_DIM, :]
                                 for t in range(BAND_BLOCKS)], axis=1), ones)], axis=0)
        out = _dot(vx, p_buf[...])
        ys = []
        for hh in range(2):
            blk = out[hh * rows_h:(hh + 1) * rows_h, hh * LANES:(hh + 1) * LANES]
            ys.append(blk[:HEAD_DIM] * (1.0 / blk[HEAD_DIM:HEAD_DIM + 1]))
        y = jnp.concatenate(ys, axis=0)
        o_ref[0, pl.ds(pl.multiple_of(i * LANES, LANES), LANES), :] = y.T.astype(BF16)

    m = scores(0)
    weights(m)
    m = scores(1)

    def step(i, m):
        output(i - 2)
        weights(m)
        return scores(i)

    m = lax.fori_loop(2, nq, step, m)
    output(nq - 2)
    weights(m)
    output(nq - 1)


def _band_attention(qT, k, vT, bias):
    B, S, _ = k.shape
    nblk = S // LANES
    assert nblk >= BAND_BLOCKS
    feat_spec = pl.BlockSpec((1, nblk, PAIR, LANES), lambda b, p: (b, 0, p, 0))
    posm_spec = pl.BlockSpec((1, S, PAIR), lambda b, p: (b, 0, p))
    return pl.pallas_call(
        functools.partial(_band_attn_kernel, nq=nblk),
        grid=(B, N_PAIRS),
        in_specs=[feat_spec, posm_spec, feat_spec,
                  pl.BlockSpec((1, BAND_BLOCKS, BAND_KEYS, 2 * LANES), lambda b, p: (p, 0, 0, 0))],
        out_specs=posm_spec,
        out_shape=jax.ShapeDtypeStruct((B, S, WIDTH_B), BF16),
        scratch_shapes=[pltpu.VMEM((BAND_KEYS, 2 * LANES), F32), pltpu.VMEM((BAND_KEYS, 2 * LANES), BF16)],
        compiler_params=pltpu.CompilerParams(
            dimension_semantics=("parallel", "parallel"), vmem_limit_bytes=VMEM_LIMIT),
        name="band_attn",
    )(qT, k, vT, bias)


def _mlp_kernel(x_ref, oa_ref, ob_ref, woa_ref, wob_ref, g2_ref, w1_ref, w2_ref, gf_ref, out_ref, *, ff_chunk):
    y = x_ref[...] + _dot(oa_ref[...], woa_ref[...]) + _dot(ob_ref[...], wob_ref[...])
    ms = jnp.mean(y * y, axis=-1, keepdims=True)
    h = ((y * lax.rsqrt(ms + EPS)) * g2_ref[...]).astype(BF16)
    mlp = None
    for c in range(D_FF // ff_chunk):
        u = _dot(h, w1_ref[:, c * ff_chunk:(c + 1) * ff_chunk])
        u = jnp.maximum(u, 0.0)
        d = _dot((u * u).astype(BF16), w2_ref[c * ff_chunk:(c + 1) * ff_chunk, :])
        mlp = d if mlp is None else mlp + d
    acc = y + mlp
    ms = jnp.mean(acc * acc, axis=-1, keepdims=True)
    out_ref[...] = (acc * lax.rsqrt(ms + EPS)) * gf_ref[...]


def _out_mlp(x2d, oa, ob, woa, wob, g2, w1, w2, gf, *, tm, ff_chunk):
    T = x2d.shape[0]
    const = lambda shape: pl.BlockSpec(shape, lambda i: (0,) * len(shape), pipeline_mode=pl.Buffered(1))
    return pl.pallas_call(
        functools.partial(_mlp_kernel, ff_chunk=ff_chunk),
        grid=(T // tm,),
        in_specs=[
            pl.BlockSpec((tm, D_MODEL), lambda i: (i, 0)),
            pl.BlockSpec((tm, WIDTH_A), lambda i: (i, 0)),
            pl.BlockSpec((tm, WIDTH_B), lambda i: (i, 0)),
            const(woa.shape), const(wob.shape), const((1, D_MODEL)),
            const(w1.shape), const(w2.shape), const((1, D_MODEL)),
        ],
        out_specs=pl.BlockSpec((tm, D_MODEL), lambda i: (i, 0)),
        out_shape=jax.ShapeDtypeStruct((T, D_MODEL), F32),
        compiler_params=pltpu.CompilerParams(
            dimension_semantics=("parallel",), vmem_limit_bytes=VMEM_LIMIT),
        name="out_mlp",
    )(x2d, oa, ob, woa, wob, g2, w1, w2, gf)


def _pair_layout(wcols):
    half = DIFF_DIM // 2
    return (wcols.reshape(-1, N_PAIRS, PAIR // DIFF_DIM, 2, half).transpose(0, 1, 3, 2, 4)
            .reshape(-1, WIDTH_A))


def _rope_tables(S):
    half = DIFF_DIM // 2
    inv_freq = ROPE_THETA ** (-jnp.arange(half, dtype=F32) / half)
    ang = jnp.arange(S).astype(F32)[:, None] * inv_freq[None, :]
    cos, sin = jnp.cos(ang), jnp.sin(ang)
    scale = DIFF_DIM ** -0.5 * LOG2E
    cqT = jnp.tile(cos.T, (HEAD_DIM // half, 1)) * scale
    sqT = jnp.tile(sin.T, (HEAD_DIM // half, 1)) * scale
    ck = jnp.tile(cos, (1, PAIR // half))
    sign = jnp.where(jnp.arange(PAIR) < HEAD_DIM, -1.0, 1.0).astype(F32)
    sk = jnp.tile(sin, (1, PAIR // half)) * sign[None, :]
    return cqT, sqT, ck, sk


def _band_bias(rel_table):
    t = jnp.arange(BAND_BLOCKS)[:, None, None]
    j = jnp.arange(BAND_KEYS)[None, :, None]
    q = t * LANES + jnp.arange(LANES)[None, None, :]
    dist = jnp.clip(q - j, -MAX_REL_DIST, MAX_REL_DIST) + MAX_REL_DIST
    dc = q // CHUNK - j // CHUNK
    ok = (dc >= 0) & (dc <= LEFT_CHUNKS)
    bias = jnp.where(ok[None], rel_table[:, dist].astype(F32) * LOG2E, NEG)
    bias = bias.reshape(N_PAIRS, 2, BAND_BLOCKS, BAND_KEYS, LANES).transpose(0, 2, 3, 1, 4)
    return bias.reshape(N_PAIRS, BAND_BLOCKS, BAND_KEYS, 2 * LANES)


def kernel(x, w_in, w_out, norm1_g, norm2_g, final_g, subln_g, lambda_q1, lambda_k1, lambda_q2, lambda_k2,
           rel_bias, w_ff1, w_ff2):
    B, S, D = x.shape
    depth = w_in.shape[0]
    assert D == D_MODEL and depth == 1 and S % 256 == 0
    l = 0
    w = w_in[l]
    c0, c1, c2, c3, c4 = np.cumsum([WIDTH_A, WIDTH_A, WIDTH_A, WIDTH_B, WIDTH_B]).tolist()
    wq_a = _pair_layout(w[:, :c0])
    wk_a = _pair_layout(w[:, c0:c1])
    wqv = jnp.concatenate([wq_a, w[:, c1:c2], w[:, c2:c3], w[:, c4:]], axis=1).T.astype(BF16)
    wk = jnp.concatenate([wk_a, w[:, c3:c4]], axis=1).astype(BF16)
    cqT, sqT, ck, sk = _rope_tables(S)

    qaT, vaT, qbT, vbT, ka, kb = _project(
        x, norm1_g[l].reshape(1, D), wqv, wk, cqT, sqT, ck, sk, tm=512)

    oa = _diff_attention(
        qaT, ka, vaT,
        lambda_q1[l].reshape(1, -1).astype(F32), lambda_k1[l].reshape(1, -1).astype(F32),
        lambda_q2[l].reshape(1, -1).astype(F32), lambda_k2[l].reshape(1, -1).astype(F32),
        subln_g[l].reshape(HEAD_DIM, 1), qb=256)
    ob = _band_attention(qbT, kb, vbT, _band_bias(rel_bias[l]))

    wo = w_out[l].astype(BF16)
    out = _out_mlp(
        x.reshape(B * S, D), oa.reshape(B * S, WIDTH_A), ob.reshape(B * S, WIDTH_B),
        wo[:WIDTH_A], wo[WIDTH_A:], norm2_g[l].reshape(1, D),
        w_ff1[l].astype(BF16), w_ff2[l].astype(BF16), final_g.reshape(1, D), tm=512, ff_chunk=1024)
    return out.reshape(B, S, D)
```

```python
import functools
import math

import jax
import jax.numpy as jnp
import numpy as np
from jax import lax
from jax.experimental import pallas as pl
from jax.experimental.pallas import tpu as pltpu

D_MODEL = 1024
HEAD_DIM = 64
DIFF_DIM = 32
N_HEADS_A = 8
N_HEADS_B = 8
WIDTH_A = N_HEADS_A * HEAD_DIM
WIDTH_B = N_HEADS_B * HEAD_DIM
CHUNK = 64
LEFT_CHUNKS = 8
MAX_REL_DIST = 256
D_FF = 4 * D_MODEL
ROPE_THETA = 10000.0
EPS = 1e-6
LAM_INIT = 0.8 - 0.6 * math.exp(-0.3 * 0)

LANES = 128
PAIR = 2 * HEAD_DIM
N_PAIRS = WIDTH_A // PAIR
VMEM_LIMIT = 56 * 1024 * 1024

BF16 = jnp.bfloat16
F32 = jnp.float32
NEG = -1e30
ONES_ROWS = 16
LOG2E = math.log2(math.e)


def _dot(a, b):
    return jnp.dot(a, b, preferred_element_type=F32)


def _dot_nt(a, b):
    return lax.dot_general(a, b, (((1,), (1,)), ((), ())), preferred_element_type=F32)


def _proj_kernel(x_ref, g_ref, wqv_ref, wk_ref, cqT_ref, sqT_ref, ck_ref, sk_ref,
                 qa_ref, va_ref, qb_ref, vb_ref, ka_ref, kb_ref, *, tm):
    x = x_ref[0]
    ms = jnp.mean(x * x, axis=-1, keepdims=True)
    h = ((x * lax.rsqrt(ms + EPS)) * g_ref[...]).astype(BF16)
    nblk = tm // LANES

    qvT = _dot_nt(wqv_ref[...], h)
    cq = cqT_ref[...]
    sq = sqT_ref[...]
    for g in range(N_PAIRS):
        x1 = qvT[g * PAIR:g * PAIR + HEAD_DIM]
        x2 = qvT[g * PAIR + HEAD_DIM:(g + 1) * PAIR]
        r1 = (x1 * cq - x2 * sq).astype(BF16)
        r2 = (x2 * cq + x1 * sq).astype(BF16)
        for t in range(nblk):
            qa_ref[0, t, g * PAIR:g * PAIR + HEAD_DIM, :] = r1[:, t * LANES:(t + 1) * LANES]
            qa_ref[0, t, g * PAIR + HEAD_DIM:(g + 1) * PAIR, :] = r2[:, t * LANES:(t + 1) * LANES]
    va = qvT[WIDTH_A:2 * WIDTH_A].astype(BF16)
    qb = (qvT[2 * WIDTH_A:2 * WIDTH_A + WIDTH_B] * (HEAD_DIM ** -0.5 * LOG2E)).astype(BF16)
    vb = qvT[2 * WIDTH_A + WIDTH_B:].astype(BF16)
    for t in range(nblk):
        va_ref[0, t] = va[:, t * LANES:(t + 1) * LANES]
        qb_ref[0, t] = qb[:, t * LANES:(t + 1) * LANES]
        vb_ref[0, t] = vb[:, t * LANES:(t + 1) * LANES]

    k = _dot(h, wk_ref[...])
    ck = ck_ref[...]
    sk = sk_ref[...]
    for g in range(N_PAIRS):
        xk = k[:, g * PAIR:(g + 1) * PAIR]
        ka_ref[0, :, g * PAIR:(g + 1) * PAIR] = (xk * ck + pltpu.roll(xk, HEAD_DIM, 1) * sk).astype(BF16)
    kb_ref[0] = k[:, WIDTH_A:].astype(BF16)


def _project(x, g1, wqv, wk, cqT, sqT, ck, sk, *, tm):
    B, S, _ = x.shape
    nblk = tm // LANES
    feat = jax.ShapeDtypeStruct((B, S // LANES, WIDTH_A, LANES), BF16)
    posm = jax.ShapeDtypeStruct((B, S, WIDTH_A), BF16)
    feat_spec = pl.BlockSpec((1, nblk, WIDTH_A, LANES), lambda b, i: (b, i, 0, 0))
    posm_spec = pl.BlockSpec((1, tm, WIDTH_A), lambda b, i: (b, i, 0))
    const = lambda shape: pl.BlockSpec(shape, lambda b, i: (0,) * len(shape), pipeline_mode=pl.Buffered(1))
    return pl.pallas_call(
        functools.partial(_proj_kernel, tm=tm),
        grid=(B, S // tm),
        in_specs=[
            pl.BlockSpec((1, tm, D_MODEL), lambda b, i: (b, i, 0)),
            const((1, D_MODEL)),
            const(wqv.shape),
            const(wk.shape),
            pl.BlockSpec((HEAD_DIM, tm), lambda b, i: (0, i)),
            pl.BlockSpec((HEAD_DIM, tm), lambda b, i: (0, i)),
            pl.BlockSpec((tm, PAIR), lambda b, i: (i, 0)),
            pl.BlockSpec((tm, PAIR), lambda b, i: (i, 0)),
        ],
        out_specs=[feat_spec, feat_spec, feat_spec, feat_spec, posm_spec, posm_spec],
        out_shape=[feat, feat, feat, feat, posm, posm],
        compiler_params=pltpu.CompilerParams(
            dimension_semantics=("parallel", "parallel"), vmem_limit_bytes=VMEM_LIMIT),
        name="proj_rope",
    )(x, g1, wqv, wk, cqT, sqT, ck, sk)


def _diff_attn_kernel(q_ref, k_ref, v_ref, lq1_ref, lk1_ref, lq2_ref, lk2_ref, g_ref, o_ref, s_buf, p_buf,
                      *, qb, nq):
    lam = (jnp.exp(jnp.sum(lq1_ref[...] * lk1_ref[...], axis=-1, keepdims=True))
           - jnp.exp(jnp.sum(lq2_ref[...] * lk2_ref[...], axis=-1, keepdims=True)) + LAM_INIT)
    gain = g_ref[...] * (1.0 - LAM_INIT)
    sub = qb // LANES
    row = lax.broadcasted_iota(jnp.int32, (PAIR, qb), 0)
    sub_head = (row % HEAD_DIM) // (DIFF_DIM // 2)
    kk = lax.broadcasted_iota(jnp.int32, (qb, qb), 0) // CHUNK
    qq = lax.broadcasted_iota(jnp.int32, (qb, qb), 1) // CHUNK
    diag_mask = jnp.where(kk <= qq, 0.0, NEG)
    ones = jnp.ones((ONES_ROWS, qb), BF16)

    def make_w(i):
        qT = jnp.concatenate([q_ref[0, i * sub + t] for t in range(sub)], axis=1)
        zero = jnp.zeros_like(qT)
        return jnp.concatenate([jnp.where(sub_head == j, qT, zero) for j in range(4)], axis=1)

    def scores(w, blk, masked, slot):
        kblk = k_ref[0, blk * qb:(blk + 1) * qb, :]
        s = _dot(kblk, w)
        if masked:
            s = s + jnp.concatenate([diag_mask] * 4, axis=1)
        s_buf[slot] = s
        return jnp.max(s, axis=0, keepdims=True)

    def weights(m_blk, m, slot):
        m_new = m_blk if m is None else jnp.maximum(m, m_blk)
        p_buf[slot] = jnp.exp2(s_buf[slot] - m_new).astype(BF16)
        return m_new, (None if m is None else jnp.exp2(m - m_new))

    def accum(blk, alpha, acc, slot):
        ds = []
        for hh in range(2):
            vx = jnp.concatenate(
                [jnp.concatenate([v_ref[0, blk * sub + t, hh * HEAD_DIM:(hh + 1) * HEAD_DIM, :]
                                  for t in range(sub)], axis=1), ones], axis=0)
            ds.append(_dot(vx, p_buf[slot, :, 2 * qb * hh:2 * qb * (hh + 1)]))
        d = jnp.concatenate(ds, axis=1)
        return d if acc is None else alpha * acc + d

    def finalize(i, acc):
        ys = []
        for hh in range(2):
            o = []
            for c in range(2):
                a = acc[:, (2 * hh + c) * qb:(2 * hh + c + 1) * qb]
                o.append(a[:HEAD_DIM] * (1.0 / a[HEAD_DIM:HEAD_DIM + 1]))
            d = o[0] - lam * o[1]
            ms = jnp.mean(d * d, axis=0, keepdims=True)
            ys.append(d * lax.rsqrt(ms + EPS) * gain)
        y = jnp.concatenate(ys, axis=0)
        o_ref[0, i * qb:(i + 1) * qb, :] = y.T.astype(BF16)

    ticks = [(i, blk) for i in range(nq) for blk in [i] + list(range(i))]
    last = {i: max(T for T, (ii, _) in enumerate(ticks) if ii == i) for i in range(nq)}
    st = {}
    for T in range(len(ticks) + 2):
        if T >= 2:
            i, blk = ticks[T - 2]
            S = st[i]
            S["acc"] = accum(blk, S["alpha"], S["acc"], T % 2)
            if T - 2 == last[i]:
                finalize(i, S["acc"])
                del st[i]
        if 1 <= T <= len(ticks):
            i, blk = ticks[T - 1]
            S = st[i]
            S["m"], S["alpha"] = weights(S["m_blk"], S["m"], (T - 1) % 2)
        if T < len(ticks):
            i, blk = ticks[T]
            if blk == i:
                st[i] = dict(w=make_w(i), m=None, alpha=None, acc=None)
            st[i]["m_blk"] = scores(st[i]["w"], blk, blk == i, T % 2)


def _diff_attention(qT, k, vT, lq1, lk1, lq2, lk2, g, *, qb):
    B, S, _ = k.shape
    nblk = S // LANES
    feat_spec = pl.BlockSpec((1, nblk, PAIR, LANES), lambda b, p: (b, 0, p, 0))
    posm_spec = pl.BlockSpec((1, S, PAIR), lambda b, p: (b, 0, p))
    vec = pl.BlockSpec((1, DIFF_DIM), lambda b, p: (0, 0))
    return pl.pallas_call(
        functools.partial(_diff_attn_kernel, qb=qb, nq=S // qb),
        grid=(B, N_PAIRS),
        in_specs=[feat_spec, posm_spec, feat_spec, vec, vec, vec, vec,
                  pl.BlockSpec((HEAD_DIM, 1), lambda b, p: (0, 0))],
        out_specs=posm_spec,
        out_shape=jax.ShapeDtypeStruct((B, S, WIDTH_A), BF16),
        scratch_shapes=[pltpu.VMEM((2, qb, 4 * qb), F32), pltpu.VMEM((2, qb, 4 * qb), BF16)],
        compiler_params=pltpu.CompilerParams(
            dimension_semantics=("parallel", "parallel"), vmem_limit_bytes=VMEM_LIMIT),
        name="diff_attn",
    )(qT, k, vT, lq1, lk1, lq2, lk2, g)


BAND_BLOCKS = LEFT_CHUNKS * CHUNK // LANES + 1
BAND_KEYS = BAND_BLOCKS * LANES


def _band_attn_kernel(q_ref, k_ref, v_ref, bias_ref, o_ref, s_buf, p_buf, *, nq):
    row = lax.broadcasted_iota(jnp.int32, (PAIR, LANES), 0)
    rows_h = HEAD_DIM + ONES_ROWS

    def span(i):
        nb = min(i + 1, BAND_BLOCKS)
        return i + 1 - nb, nb

    def scores(i, slot):
        lo, nb = span(i)
        qT = q_ref[0, i]
        zero = jnp.zeros_like(qT)
        w = jnp.concatenate([jnp.where(row < HEAD_DIM, qT, zero),
                             jnp.where(row >= HEAD_DIM, qT, zero)], axis=1)
        kblk = k_ref[0, lo * LANES:(lo + nb) * LANES, :]
        s = _dot(kblk, w) + bias_ref[0, (BAND_BLOCKS - nb) * LANES:, :]
        s_buf[slot, 0:nb * LANES, :] = s
        return jnp.max(s, axis=0, keepdims=True)

    def weights(i, m, slot):
        _, nb = span(i)
        p_buf[slot, 0:nb * LANES, :] = jnp.exp2(s_buf[slot, 0:nb * LANES, :] - m).astype(BF16)

    def output(i, slot):
        lo, nb = span(i)
        ones = jnp.ones((ONES_ROWS, nb * LANES), BF16)
        vx = jnp.concatenate(
            [part for hh in range(2) for part in (
                jnp.concatenate([v_ref[0, lo + t, hh * HEAD_DIM:(hh + 1) * HEAD_DIM, :]
                                 for t in range(nb)], axis=1), ones)], axis=0)
        out = _dot(vx, p_buf[slot, 0:nb * LANES, :])
        ys = []
        for hh in range(2):
            blk = out[hh * rows_h:(hh + 1) * rows_h, hh * LANES:(hh + 1) * LANES]
            ys.append(blk[:HEAD_DIM] * (1.0 / blk[HEAD_DIM:HEAD_DIM + 1]))
        y = jnp.concatenate(ys, axis=0)
        o_ref[0, i * LANES:(i + 1) * LANES, :] = y.T.astype(BF16)

    m = {}
    for i in range(nq + 2):
        if i >= 2:
            output(i - 2, i % 2)
        if 1 <= i <= nq:
            weights(i - 1, m.pop(i - 1), (i - 1) % 2)
        if i < nq:
            m[i] = scores(i, i % 2)


def _band_attention(qT, k, vT, bias):
    B, S, _ = k.shape
    nblk = S // LANES
    feat_spec = pl.BlockSpec((1, nblk, PAIR, LANES), lambda b, p: (b, 0, p, 0))
    posm_spec = pl.BlockSpec((1, S, PAIR), lambda b, p: (b, 0, p))
    return pl.pallas_call(
        functools.partial(_band_attn_kernel, nq=nblk),
        grid=(B, N_PAIRS),
        in_specs=[feat_spec, posm_spec, feat_spec,
                  pl.BlockSpec((1, BAND_KEYS, 2 * LANES), lambda b, p: (p, 0, 0))],
        out_specs=posm_spec,
        out_shape=jax.ShapeDtypeStruct((B, S, WIDTH_B), BF16),
        scratch_shapes=[pltpu.VMEM((2, BAND_KEYS, 2 * LANES), F32), pltpu.VMEM((2, BAND_KEYS, 2 * LANES), BF16)],
        compiler_params=pltpu.CompilerParams(
            dimension_semantics=("parallel", "parallel"), vmem_limit_bytes=VMEM_LIMIT),
        name="band_attn",
    )(qT, k, vT, bias)


def _mlp_kernel(x_ref, oa_ref, ob_ref, woa_ref, wob_ref, g2_ref, w1_ref, w2_ref, gf_ref, out_ref, *, ff_chunk):
    y = x_ref[...] + _dot(oa_ref[...], woa_ref[...]) + _dot(ob_ref[...], wob_ref[...])
    ms = jnp.mean(y * y, axis=-1, keepdims=True)
    h = ((y * lax.rsqrt(ms + EPS)) * g2_ref[...]).astype(BF16)
    mlp = None
    for c in range(D_FF // ff_chunk):
        u = _dot(h, w1_ref[:, c * ff_chunk:(c + 1) * ff_chunk])
        u = jnp.maximum(u, 0.0)
        d = _dot((u * u).astype(BF16), w2_ref[c * ff_chunk:(c + 1) * ff_chunk, :])
        mlp = d if mlp is None else mlp + d
    acc = y + mlp
    ms = jnp.mean(acc * acc, axis=-1, keepdims=True)
    out_ref[...] = (acc * lax.rsqrt(ms + EPS)) * gf_ref[...]


def _out_mlp(x2d, oa, ob, woa, wob, g2, w1, w2, gf, *, tm, ff_chunk):
    T = x2d.shape[0]
    const = lambda shape: pl.BlockSpec(shape, lambda i: (0,) * len(shape), pipeline_mode=pl.Buffered(1))
    return pl.pallas_call(
        functools.partial(_mlp_kernel, ff_chunk=ff_chunk),
        grid=(T // tm,),
        in_specs=[
            pl.BlockSpec((tm, D_MODEL), lambda i: (i, 0)),
            pl.BlockSpec((tm, WIDTH_A), lambda i: (i, 0)),
            pl.BlockSpec((tm, WIDTH_B), lambda i: (i, 0)),
            const(woa.shape), const(wob.shape), const((1, D_MODEL)),
            const(w1.shape), const(w2.shape), const((1, D_MODEL)),
        ],
        out_specs=pl.BlockSpec((tm, D_MODEL), lambda i: (i, 0)),
        out_shape=jax.ShapeDtypeStruct((T, D_MODEL), F32),
        compiler_params=pltpu.CompilerParams(
            dimension_semantics=("parallel",), vmem_limit_bytes=VMEM_LIMIT),
        name="out_mlp",
    )(x2d, oa, ob, woa, wob, g2, w1, w2, gf)


def _pair_layout(wcols):
    half = DIFF_DIM // 2
    return (wcols.reshape(-1, N_PAIRS, PAIR // DIFF_DIM, 2, half).transpose(0, 1, 3, 2, 4)
            .reshape(-1, WIDTH_A))


def _rope_tables(S):
    half = DIFF_DIM // 2
    inv_freq = ROPE_THETA ** (-jnp.arange(half, dtype=F32) / half)
    ang = jnp.arange(S).astype(F32)[:, None] * inv_freq[None, :]
    cos, sin = jnp.cos(ang), jnp.sin(ang)
    scale = DIFF_DIM ** -0.5 * LOG2E
    cqT = jnp.tile(cos.T, (HEAD_DIM // half, 1)) * scale
    sqT = jnp.tile(sin.T, (HEAD_DIM // half, 1)) * scale
    ck = jnp.tile(cos, (1, PAIR // half))
    sign = jnp.where(jnp.arange(PAIR) < HEAD_DIM, -1.0, 1.0).astype(F32)
    sk = jnp.tile(sin, (1, PAIR // half)) * sign[None, :]
    return cqT, sqT, ck, sk


def _band_bias(rel_table):
    H = rel_table.shape[0]
    span, width = BAND_KEYS, 2 * BAND_KEYS
    back = LEFT_CHUNKS * CHUNK
    tbl = rel_table.astype(F32) * LOG2E
    left = span - MAX_REL_DIST
    right = width + 1 - left - tbl.shape[1]
    ext = jnp.concatenate([jnp.broadcast_to(tbl[:, :1], (H, left)), tbl,
                           jnp.broadcast_to(tbl[:, -1:], (H, right))], axis=1)
    toe = jnp.tile(ext, (1, span))[:, :span * width].reshape(H, span, width)
    vals = toe[:, :, span + back:span + back + LANES]
    j = jnp.arange(span)[:, None]
    q = back + jnp.arange(LANES)[None, :]
    dc = q // CHUNK - j // CHUNK
    ok = (dc >= 0) & (dc <= LEFT_CHUNKS)
    bias = jnp.where(ok[None], vals, NEG)
    return bias.reshape(N_PAIRS, 2, span, LANES).transpose(0, 2, 1, 3).reshape(N_PAIRS, span, 2 * LANES)


def kernel(x, w_in, w_out, norm1_g, norm2_g, final_g, subln_g, lambda_q1, lambda_k1, lambda_q2, lambda_k2,
           rel_bias, w_ff1, w_ff2):
    B, S, D = x.shape
    depth = w_in.shape[0]
    assert D == D_MODEL and depth == 1 and S % 256 == 0
    l = 0
    w = w_in[l]
    c0, c1, c2, c3, c4 = np.cumsum([WIDTH_A, WIDTH_A, WIDTH_A, WIDTH_B, WIDTH_B]).tolist()
    wq_a = _pair_layout(w[:, :c0])
    wk_a = _pair_layout(w[:, c0:c1])
    wqv = jnp.concatenate([wq_a, w[:, c1:c2], w[:, c2:c3], w[:, c4:]], axis=1).T.astype(BF16)
    wk = jnp.concatenate([wk_a, w[:, c3:c4]], axis=1).astype(BF16)
    cqT, sqT, ck, sk = _rope_tables(S)

    qaT, vaT, qbT, vbT, ka, kb = _project(
        x, norm1_g[l].reshape(1, D), wqv, wk, cqT, sqT, ck, sk, tm=512)

    oa = _diff_attention(
        qaT, ka, vaT,
        lambda_q1[l].reshape(1, -1).astype(F32), lambda_k1[l].reshape(1, -1).astype(F32),
        lambda_q2[l].reshape(1, -1).astype(F32), lambda_k2[l].reshape(1, -1).astype(F32),
        subln_g[l].reshape(HEAD_DIM, 1), qb=256)
    ob = _band_attention(qbT, kb, vbT, _band_bias(rel_bias[l]))

    wo = w_out[l].astype(BF16)
    out = _out_mlp(
        x.reshape(B * S, D), oa.reshape(B * S, WIDTH_A), ob.reshape(B * S, WIDTH_B),
        wo[:WIDTH_A], wo[WIDTH_A:], norm2_g[l].reshape(1, D),
        w_ff1[l].astype(BF16), w_ff2[l].astype(BF16), final_g.reshape(1, D), tm=512, ff_chunk=1024)
    return out.reshape(B, S, D)
```

```python
import functools
import math

import jax
import jax.numpy as jnp
import numpy as np
from jax import lax
from jax.experimental import pallas as pl
from jax.experimental.pallas import tpu as pltpu

D_MODEL = 1024
HEAD_DIM = 64
DIFF_DIM = 32
N_HEADS_A = 8
N_HEADS_B = 8
WIDTH_A = N_HEADS_A * HEAD_DIM
WIDTH_B = N_HEADS_B * HEAD_DIM
CHUNK = 64
LEFT_CHUNKS = 8
MAX_REL_DIST = 256
D_FF = 4 * D_MODEL
ROPE_THETA = 10000.0
EPS = 1e-6
LAM_INIT = 0.8 - 0.6 * math.exp(-0.3 * 0)

LANES = 128
PAIR = 2 * HEAD_DIM
N_PAIRS = WIDTH_A // PAIR
VMEM_LIMIT = 56 * 1024 * 1024

BF16 = jnp.bfloat16
F32 = jnp.float32
NEG = -1e30
ONES_ROWS = 16
LOG2E = math.log2(math.e)


def _dot(a, b):
    return jnp.dot(a, b, preferred_element_type=F32)


def _dot_nt(a, b):
    return lax.dot_general(a, b, (((1,), (1,)), ((), ())), preferred_element_type=F32)


def _proj_kernel(x_ref, g_ref, wqv_ref, wk_ref, cqT_ref, sqT_ref, ck_ref, sk_ref,
                 qa_ref, va_ref, qb_ref, vb_ref, ka_ref, kb_ref, *, tm):
    x = x_ref[0]
    ms = jnp.mean(x * x, axis=-1, keepdims=True)
    h = ((x * lax.rsqrt(ms + EPS)) * g_ref[...]).astype(BF16)
    nblk = tm // LANES

    qvT = _dot_nt(wqv_ref[...], h)
    cq = cqT_ref[...]
    sq = sqT_ref[...]
    for g in range(N_PAIRS):
        x1 = qvT[g * PAIR:g * PAIR + HEAD_DIM]
        x2 = qvT[g * PAIR + HEAD_DIM:(g + 1) * PAIR]
        r1 = (x1 * cq - x2 * sq).astype(BF16)
        r2 = (x2 * cq + x1 * sq).astype(BF16)
        for t in range(nblk):
            qa_ref[0, t, g * PAIR:g * PAIR + HEAD_DIM, :] = r1[:, t * LANES:(t + 1) * LANES]
            qa_ref[0, t, g * PAIR + HEAD_DIM:(g + 1) * PAIR, :] = r2[:, t * LANES:(t + 1) * LANES]
    va = qvT[WIDTH_A:2 * WIDTH_A].astype(BF16)
    qb = (qvT[2 * WIDTH_A:2 * WIDTH_A + WIDTH_B] * (HEAD_DIM ** -0.5 * LOG2E)).astype(BF16)
    vb = qvT[2 * WIDTH_A + WIDTH_B:].astype(BF16)
    for t in range(nblk):
        va_ref[0, t] = va[:, t * LANES:(t + 1) * LANES]
        qb_ref[0, t] = qb[:, t * LANES:(t + 1) * LANES]
        vb_ref[0, t] = vb[:, t * LANES:(t + 1) * LANES]

    k = _dot(h, wk_ref[...])
    ck = ck_ref[...]
    sk = sk_ref[...]
    for g in range(N_PAIRS):
        xk = k[:, g * PAIR:(g + 1) * PAIR]
        ka_ref[0, :, g * PAIR:(g + 1) * PAIR] = (xk * ck + pltpu.roll(xk, HEAD_DIM, 1) * sk).astype(BF16)
    kb_ref[0] = k[:, WIDTH_A:].astype(BF16)


def _project(x, g1, wqv, wk, cqT, sqT, ck, sk, *, tm):
    B, S, _ = x.shape
    nblk = tm // LANES
    feat = jax.ShapeDtypeStruct((B, S // LANES, WIDTH_A, LANES), BF16)
    posm = jax.ShapeDtypeStruct((B, S, WIDTH_A), BF16)
    feat_spec = pl.BlockSpec((1, nblk, WIDTH_A, LANES), lambda b, i: (b, i, 0, 0))
    posm_spec = pl.BlockSpec((1, tm, WIDTH_A), lambda b, i: (b, i, 0))
    const = lambda shape: pl.BlockSpec(shape, lambda b, i: (0,) * len(shape), pipeline_mode=pl.Buffered(1))
    return pl.pallas_call(
        functools.partial(_proj_kernel, tm=tm),
        grid=(B, S // tm),
        in_specs=[
            pl.BlockSpec((1, tm, D_MODEL), lambda b, i: (b, i, 0)),
            const((1, D_MODEL)),
            const(wqv.shape),
            const(wk.shape),
            pl.BlockSpec((HEAD_DIM, tm), lambda b, i: (0, i)),
            pl.BlockSpec((HEAD_DIM, tm), lambda b, i: (0, i)),
            pl.BlockSpec((tm, PAIR), lambda b, i: (i, 0)),
            pl.BlockSpec((tm, PAIR), lambda b, i: (i, 0)),
        ],
        out_specs=[feat_spec, feat_spec, feat_spec, feat_spec, posm_spec, posm_spec],
        out_shape=[feat, feat, feat, feat, posm, posm],
        compiler_params=pltpu.CompilerParams(
            dimension_semantics=("parallel", "parallel"), vmem_limit_bytes=VMEM_LIMIT),
        name="proj_rope",
    )(x, g1, wqv, wk, cqT, sqT, ck, sk)


def _diff_attn_kernel(q_ref, k_ref, v_ref, lq1_ref, lk1_ref, lq2_ref, lk2_ref, g_ref, o_ref, s_buf, p_buf,
                      *, qb, nq):
    lam = (jnp.exp(jnp.sum(lq1_ref[...] * lk1_ref[...], axis=-1, keepdims=True))
           - jnp.exp(jnp.sum(lq2_ref[...] * lk2_ref[...], axis=-1, keepdims=True)) + LAM_INIT)
    gain = g_ref[...] * (1.0 - LAM_INIT)
    sub = qb // LANES
    row = lax.broadcasted_iota(jnp.int32, (PAIR, qb), 0)
    sub_head = (row % HEAD_DIM) // (DIFF_DIM // 2)
    kk = lax.broadcasted_iota(jnp.int32, (qb, qb), 0) // CHUNK
    qq = lax.broadcasted_iota(jnp.int32, (qb, qb), 1) // CHUNK
    diag_mask = jnp.where(kk <= qq, 0.0, NEG)
    ones = jnp.ones((ONES_ROWS, qb), BF16)

    NG = 4

    def make_w(i):
        qT = jnp.concatenate([q_ref[0, i * sub + t] for t in range(sub)], axis=1)
        zero = jnp.zeros_like(qT)
        return [jnp.where(sub_head == c, qT, zero) for c in range(NG)]

    def scores(w, blk, masked, slot, c):
        kblk = k_ref[0, blk * qb:(blk + 1) * qb, :]
        s = _dot(kblk, w[c])
        if masked:
            s = s + diag_mask
        s_buf[slot, :, c * qb:(c + 1) * qb] = s
        return jnp.max(s, axis=0, keepdims=True)

    def weights(m_blk, m, slot, c):
        m_new = m_blk if m is None else jnp.maximum(m, m_blk)
        p_buf[slot, :, c * qb:(c + 1) * qb] = jnp.exp2(s_buf[slot, :, c * qb:(c + 1) * qb] - m_new).astype(BF16)
        return m_new, (None if m is None else jnp.exp2(m - m_new))

    def accum(blk, alpha, acc, slot, c):
        hh = c // 2
        vx = jnp.concatenate(
            [jnp.concatenate([v_ref[0, blk * sub + t, hh * HEAD_DIM:(hh + 1) * HEAD_DIM, :]
                              for t in range(sub)], axis=1), ones], axis=0)
        d = _dot(vx, p_buf[slot, :, c * qb:(c + 1) * qb])
        return d if acc is None else alpha * acc + d

    def finalize(i, accs):
        ys = []
        for hh in range(2):
            o = []
            for c in range(2):
                a = accs[2 * hh + c]
                o.append(a[:HEAD_DIM] * (1.0 / a[HEAD_DIM:HEAD_DIM + 1]))
            d = o[0] - lam * o[1]
            ms = jnp.mean(d * d, axis=0, keepdims=True)
            ys.append(d * lax.rsqrt(ms + EPS) * gain)
        y = jnp.concatenate(ys, axis=0)
        o_ref[0, i * qb:(i + 1) * qb, :] = y.T.astype(BF16)

    ticks = [(i, blk) for i in range(nq) for blk in [i] + list(range(i))]
    last = {i: max(T for T, (ii, _) in enumerate(ticks) if ii == i) for i in range(nq)}
    st = {}
    for T in range(len(ticks) + 2):
        if T < len(ticks) and ticks[T][1] == ticks[T][0]:
            i = ticks[T][0]
            st[i] = dict(w=make_w(i), m=[None] * NG, alpha=[None] * NG, acc=[None] * NG, m_blk=[None] * NG)
        for c in range(NG):
            if T >= 2:
                i, blk = ticks[T - 2]
                S = st[i]
                S["acc"][c] = accum(blk, S["alpha"][c], S["acc"][c], T % 2, c)
            if 1 <= T <= len(ticks):
                i, blk = ticks[T - 1]
                S = st[i]
                S["m"][c], S["alpha"][c] = weights(S["m_blk"][c], S["m"][c], (T - 1) % 2, c)
            if T < len(ticks):
                i, blk = ticks[T]
                st[i]["m_blk"][c] = scores(st[i]["w"], blk, blk == i, T % 2, c)
        if T >= 2:
            i, blk = ticks[T - 2]
            if T - 2 == last[i]:
                finalize(i, st[i]["acc"])
                del st[i]


def _diff_attention(qT, k, vT, lq1, lk1, lq2, lk2, g, *, qb):
    B, S, _ = k.shape
    nblk = S // LANES
    feat_spec = pl.BlockSpec((1, nblk, PAIR, LANES), lambda b, p: (b, 0, p, 0))
    posm_spec = pl.BlockSpec((1, S, PAIR), lambda b, p: (b, 0, p))
    vec = pl.BlockSpec((1, DIFF_DIM), lambda b, p: (0, 0))
    return pl.pallas_call(
        functools.partial(_diff_attn_kernel, qb=qb, nq=S // qb),
        grid=(B, N_PAIRS),
        in_specs=[feat_spec, posm_spec, feat_spec, vec, vec, vec, vec,
                  pl.BlockSpec((HEAD_DIM, 1), lambda b, p: (0, 0))],
        out_specs=posm_spec,
        out_shape=jax.ShapeDtypeStruct((B, S, WIDTH_A), BF16),
        scratch_shapes=[pltpu.VMEM((2, qb, 4 * qb), F32), pltpu.VMEM((2, qb, 4 * qb), BF16)],
        compiler_params=pltpu.CompilerParams(
            dimension_semantics=("parallel", "parallel"), vmem_limit_bytes=VMEM_LIMIT),
        name="diff_attn",
    )(qT, k, vT, lq1, lk1, lq2, lk2, g)


BAND_BLOCKS = LEFT_CHUNKS * CHUNK // LANES + 1
BAND_KEYS = BAND_BLOCKS * LANES


def _band_attn_kernel(q_ref, k_ref, v_ref, bias_ref, o_ref, s_buf, p_buf, *, nq):
    row = lax.broadcasted_iota(jnp.int32, (PAIR, LANES), 0)
    rows_h = HEAD_DIM + ONES_ROWS

    GROUP = 2

    def span(i):
        nb = min(i + 1, BAND_BLOCKS)
        return i + 1 - nb, nb

    def groups(i):
        nb = span(i)[1]
        return [(g, min(g + GROUP, nb)) for g in range(0, nb, GROUP)]

    def make_w(i):
        qT = q_ref[0, i]
        zero = jnp.zeros_like(qT)
        return jnp.concatenate([jnp.where(row < HEAD_DIM, qT, zero),
                                jnp.where(row >= HEAD_DIM, qT, zero)], axis=1)

    def scores(i, w, slot, g0, g1):
        lo, nb = span(i)
        kblk = k_ref[0, (lo + g0) * LANES:(lo + g1) * LANES, :]
        s = _dot(kblk, w) + bias_ref[0, (BAND_BLOCKS - nb + g0) * LANES:(BAND_BLOCKS - nb + g1) * LANES, :]
        s_buf[slot, g0 * LANES:g1 * LANES, :] = s
        return jnp.max(s, axis=0, keepdims=True)

    def weights(m, slot, g0, g1):
        p_buf[slot, g0 * LANES:g1 * LANES, :] = jnp.exp2(s_buf[slot, g0 * LANES:g1 * LANES, :] - m).astype(BF16)

    def partial_out(i, slot, g0, g1):
        lo, nb = span(i)
        ones = jnp.ones((ONES_ROWS, (g1 - g0) * LANES), BF16)
        vx = jnp.concatenate(
            [part for hh in range(2) for part in (
                jnp.concatenate([v_ref[0, lo + t, hh * HEAD_DIM:(hh + 1) * HEAD_DIM, :]
                                 for t in range(g0, g1)], axis=1), ones)], axis=0)
        return _dot(vx, p_buf[slot, g0 * LANES:g1 * LANES, :])

    def store(i, out):
        ys = []
        for hh in range(2):
            blk = out[hh * rows_h:(hh + 1) * rows_h, hh * LANES:(hh + 1) * LANES]
            ys.append(blk[:HEAD_DIM] * (1.0 / blk[HEAD_DIM:HEAD_DIM + 1]))
        y = jnp.concatenate(ys, axis=0)
        o_ref[0, i * LANES:(i + 1) * LANES, :] = y.T.astype(BF16)

    m = {}
    for i in range(nq + 2):
        w = make_w(i) if i < nq else None
        out, mg = None, None
        ng = max(len(groups(j)) for j in (i - 2, i - 1, i) if 0 <= j < nq)
        for n in range(ng):
            if i >= 2 and n < len(groups(i - 2)):
                d = partial_out(i - 2, i % 2, *groups(i - 2)[n])
                out = d if out is None else out + d
            if 1 <= i <= nq and n < len(groups(i - 1)):
                weights(m[i - 1], (i - 1) % 2, *groups(i - 1)[n])
            if i < nq and n < len(groups(i)):
                mm = scores(i, w, i % 2, *groups(i)[n])
                mg = mm if mg is None else jnp.maximum(mg, mm)
        if i >= 2:
            store(i - 2, out)
        if i < nq:
            m[i] = mg


def _band_attention(qT, k, vT, bias):
    B, S, _ = k.shape
    nblk = S // LANES
    feat_spec = pl.BlockSpec((1, nblk, PAIR, LANES), lambda b, p: (b, 0, p, 0))
    posm_spec = pl.BlockSpec((1, S, PAIR), lambda b, p: (b, 0, p))
    return pl.pallas_call(
        functools.partial(_band_attn_kernel, nq=nblk),
        grid=(B, N_PAIRS),
        in_specs=[feat_spec, posm_spec, feat_spec,
                  pl.BlockSpec((1, BAND_KEYS, 2 * LANES), lambda b, p: (p, 0, 0))],
        out_specs=posm_spec,
        out_shape=jax.ShapeDtypeStruct((B, S, WIDTH_B), BF16),
        scratch_shapes=[pltpu.VMEM((2, BAND_KEYS, 2 * LANES), F32), pltpu.VMEM((2, BAND_KEYS, 2 * LANES), BF16)],
        compiler_params=pltpu.CompilerParams(
            dimension_semantics=("parallel", "parallel"), vmem_limit_bytes=VMEM_LIMIT),
        name="band_attn",
    )(qT, k, vT, bias)


def _mlp_kernel(x_ref, oa_ref, ob_ref, woa_ref, wob_ref, g2_ref, w1_ref, w2_ref, gf_ref, out_ref, *, ff_chunk):
    y = x_ref[...] + _dot(oa_ref[...], woa_ref[...]) + _dot(ob_ref[...], wob_ref[...])
    ms = jnp.mean(y * y, axis=-1, keepdims=True)
    h = ((y * lax.rsqrt(ms + EPS)) * g2_ref[...]).astype(BF16)
    mlp = None
    for c in range(D_FF // ff_chunk):
        u = _dot(h, w1_ref[:, c * ff_chunk:(c + 1) * ff_chunk])
        u = jnp.maximum(u, 0.0)
        d = _dot((u * u).astype(BF16), w2_ref[c * ff_chunk:(c + 1) * ff_chunk, :])
        mlp = d if mlp is None else mlp + d
    acc = y + mlp
    ms = jnp.mean(acc * acc, axis=-1, keepdims=True)
    out_ref[...] = (acc * lax.rsqrt(ms + EPS)) * gf_ref[...]


def _out_mlp(x2d, oa, ob, woa, wob, g2, w1, w2, gf, *, tm, ff_chunk):
    T = x2d.shape[0]
    const = lambda shape: pl.BlockSpec(shape, lambda i: (0,) * len(shape), pipeline_mode=pl.Buffered(1))
    return pl.pallas_call(
        functools.partial(_mlp_kernel, ff_chunk=ff_chunk),
        grid=(T // tm,),
        in_specs=[
            pl.BlockSpec((tm, D_MODEL), lambda i: (i, 0)),
            pl.BlockSpec((tm, WIDTH_A), lambda i: (i, 0)),
            pl.BlockSpec((tm, WIDTH_B), lambda i: (i, 0)),
            const(woa.shape), const(wob.shape), const((1, D_MODEL)),
            const(w1.shape), const(w2.shape), const((1, D_MODEL)),
        ],
        out_specs=pl.BlockSpec((tm, D_MODEL), lambda i: (i, 0)),
        out_shape=jax.ShapeDtypeStruct((T, D_MODEL), F32),
        compiler_params=pltpu.CompilerParams(
            dimension_semantics=("parallel",), vmem_limit_bytes=VMEM_LIMIT),
        name="out_mlp",
    )(x2d, oa, ob, woa, wob, g2, w1, w2, gf)


def _pair_layout(wcols):
    half = DIFF_DIM // 2
    return (wcols.reshape(-1, N_PAIRS, PAIR // DIFF_DIM, 2, half).transpose(0, 1, 3, 2, 4)
            .reshape(-1, WIDTH_A))


def _rope_tables(S):
    half = DIFF_DIM // 2
    inv_freq = ROPE_THETA ** (-jnp.arange(half, dtype=F32) / half)
    ang = jnp.arange(S).astype(F32)[:, None] * inv_freq[None, :]
    cos, sin = jnp.cos(ang), jnp.sin(ang)
    scale = DIFF_DIM ** -0.5 * LOG2E
    cqT = jnp.tile(cos.T, (HEAD_DIM // half, 1)) * scale
    sqT = jnp.tile(sin.T, (HEAD_DIM // half, 1)) * scale
    ck = jnp.tile(cos, (1, PAIR // half))
    sign = jnp.where(jnp.arange(PAIR) < HEAD_DIM, -1.0, 1.0).astype(F32)
    sk = jnp.tile(sin, (1, PAIR // half)) * sign[None, :]
    return cqT, sqT, ck, sk


def _band_bias(rel_table):
    H = rel_table.shape[0]
    span, width = BAND_KEYS, 2 * BAND_KEYS
    back = LEFT_CHUNKS * CHUNK
    tbl = rel_table.astype(F32) * LOG2E
    left = span - MAX_REL_DIST
    right = width + 1 - left - tbl.shape[1]
    ext = jnp.concatenate([jnp.broadcast_to(tbl[:, :1], (H, left)), tbl,
                           jnp.broadcast_to(tbl[:, -1:], (H, right))], axis=1)
    vals = jnp.stack([ext[:, span + back - j:span + back - j + LANES] for j in range(span)], axis=1)
    j = jnp.arange(span)[:, None]
    q = back + jnp.arange(LANES)[None, :]
    dc = q // CHUNK - j // CHUNK
    ok = (dc >= 0) & (dc <= LEFT_CHUNKS)
    bias = jnp.where(ok[None], vals, NEG)
    return bias.reshape(N_PAIRS, 2, span, LANES).transpose(0, 2, 1, 3).reshape(N_PAIRS, span, 2 * LANES)


def kernel(x, w_in, w_out, norm1_g, norm2_g, final_g, subln_g, lambda_q1, lambda_k1, lambda_q2, lambda_k2,
           rel_bias, w_ff1, w_ff2):
    B, S, D = x.shape
    depth = w_in.shape[0]
    assert D == D_MODEL and depth == 1 and S % 256 == 0
    l = 0
    w = w_in[l]
    c0, c1, c2, c3, c4 = np.cumsum([WIDTH_A, WIDTH_A, WIDTH_A, WIDTH_B, WIDTH_B]).tolist()
    wq_a = _pair_layout(w[:, :c0])
    wk_a = _pair_layout(w[:, c0:c1])
    wqv = jnp.concatenate([wq_a, w[:, c1:c2], w[:, c2:c3], w[:, c4:]], axis=1).T.astype(BF16)
    wk = jnp.concatenate([wk_a, w[:, c3:c4]], axis=1).astype(BF16)
    cqT, sqT, ck, sk = _rope_tables(S)

    qaT, vaT, qbT, vbT, ka, kb = _project(
        x, norm1_g[l].reshape(1, D), wqv, wk, cqT, sqT, ck, sk, tm=512)

    oa = _diff_attention(
        qaT, ka, vaT,
        lambda_q1[l].reshape(1, -1).astype(F32), lambda_k1[l].reshape(1, -1).astype(F32),
        lambda_q2[l].reshape(1, -1).astype(F32), lambda_k2[l].reshape(1, -1).astype(F32),
        subln_g[l].reshape(HEAD_DIM, 1), qb=256)
    ob = _band_attention(qbT, kb, vbT, _band_bias(rel_bias[l]))

    wo = w_out[l].astype(BF16)
    out = _out_mlp(
        x.reshape(B * S, D), oa.reshape(B * S, WIDTH_A), ob.reshape(B * S, WIDTH_B),
        wo[:WIDTH_A], wo[WIDTH_A:], norm2_g[l].reshape(1, D),
        w_ff1[l].astype(BF16), w_ff2[l].astype(BF16), final_g.reshape(1, D), tm=512, ff_chunk=1024)
    return out.reshape(B, S, D)
```

```python
import functools
import math

import jax
import jax.numpy as jnp
import numpy as np
from jax import lax
from jax.experimental import pallas as pl
from jax.experimental.pallas import tpu as pltpu

D_MODEL = 1024
HEAD_DIM = 64
DIFF_DIM = 32
N_HEADS_A = 8
N_HEADS_B = 8
WIDTH_A = N_HEADS_A * HEAD_DIM
WIDTH_B = N_HEADS_B * HEAD_DIM
CHUNK = 64
LEFT_CHUNKS = 8
MAX_REL_DIST = 256
D_FF = 4 * D_MODEL
ROPE_THETA = 10000.0
EPS = 1e-6
LAM_INIT = 0.8 - 0.6 * math.exp(-0.3 * 0)

LANES = 128
PAIR = 2 * HEAD_DIM
N_PAIRS = WIDTH_A // PAIR
VMEM_LIMIT = 56 * 1024 * 1024

PROJ_ROWS = 1024
MLP_ROWS = 512
MLP_FF_CHUNK = 1024
DIFF_BLOCK = 256

BF16 = jnp.bfloat16
F32 = jnp.float32
NEG = -1e30
ONES_ROWS = 16
LOG2E = math.log2(math.e)


def _dot(a, b):
    return jnp.dot(a, b, preferred_element_type=F32)


def _dot_nt(a, b):
    return lax.dot_general(a, b, (((1,), (1,)), ((), ())), preferred_element_type=F32)


def _proj_kernel(x_ref, g_ref, wqv_ref, wk_ref, cqT_ref, sqT_ref, ck_ref, sk_ref,
                 qa_ref, va_ref, qb_ref, vb_ref, ka_ref, kb_ref, *, tm):
    x = x_ref[0]
    ms = jnp.mean(x * x, axis=-1, keepdims=True)
    h = ((x * lax.rsqrt(ms + EPS)) * g_ref[...]).astype(BF16)
    nblk = tm // LANES

    qvT = _dot_nt(wqv_ref[...], h)
    cq = cqT_ref[...]
    sq = sqT_ref[...]
    for g in range(N_PAIRS):
        x1 = qvT[g * PAIR:g * PAIR + HEAD_DIM]
        x2 = qvT[g * PAIR + HEAD_DIM:(g + 1) * PAIR]
        r1 = (x1 * cq - x2 * sq).astype(BF16)
        r2 = (x2 * cq + x1 * sq).astype(BF16)
        for t in range(nblk):
            qa_ref[0, t, g * PAIR:g * PAIR + HEAD_DIM, :] = r1[:, t * LANES:(t + 1) * LANES]
            qa_ref[0, t, g * PAIR + HEAD_DIM:(g + 1) * PAIR, :] = r2[:, t * LANES:(t + 1) * LANES]
    va = qvT[WIDTH_A:2 * WIDTH_A].astype(BF16)
    qb = (qvT[2 * WIDTH_A:2 * WIDTH_A + WIDTH_B] * (HEAD_DIM ** -0.5 * LOG2E)).astype(BF16)
    vb = qvT[2 * WIDTH_A + WIDTH_B:].astype(BF16)
    for t in range(nblk):
        va_ref[0, t] = va[:, t * LANES:(t + 1) * LANES]
        qb_ref[0, t] = qb[:, t * LANES:(t + 1) * LANES]
        vb_ref[0, t] = vb[:, t * LANES:(t + 1) * LANES]

    k = _dot(h, wk_ref[...])
    ck = ck_ref[...]
    sk = sk_ref[...]
    for g in range(N_PAIRS):
        xk = k[:, g * PAIR:(g + 1) * PAIR]
        ka_ref[0, :, g * PAIR:(g + 1) * PAIR] = (xk * ck + pltpu.roll(xk, HEAD_DIM, 1) * sk).astype(BF16)
    kb_ref[0] = k[:, WIDTH_A:].astype(BF16)


def _project(x, g1, wqv, wk, cqT, sqT, ck, sk, *, tm):
    B, S, _ = x.shape
    nblk = tm // LANES
    feat = jax.ShapeDtypeStruct((B, S // LANES, WIDTH_A, LANES), BF16)
    posm = jax.ShapeDtypeStruct((B, S, WIDTH_A), BF16)
    feat_spec = pl.BlockSpec((1, nblk, WIDTH_A, LANES), lambda b, i: (b, i, 0, 0))
    posm_spec = pl.BlockSpec((1, tm, WIDTH_A), lambda b, i: (b, i, 0))
    const = lambda shape: pl.BlockSpec(shape, lambda b, i: (0,) * len(shape), pipeline_mode=pl.Buffered(1))
    return pl.pallas_call(
        functools.partial(_proj_kernel, tm=tm),
        grid=(B, S // tm),
        in_specs=[
            pl.BlockSpec((1, tm, D_MODEL), lambda b, i: (b, i, 0)),
            const((1, D_MODEL)),
            const(wqv.shape),
            const(wk.shape),
            pl.BlockSpec((HEAD_DIM, tm), lambda b, i: (0, i)),
            pl.BlockSpec((HEAD_DIM, tm), lambda b, i: (0, i)),
            pl.BlockSpec((tm, PAIR), lambda b, i: (i, 0)),
            pl.BlockSpec((tm, PAIR), lambda b, i: (i, 0)),
        ],
        out_specs=[feat_spec, feat_spec, feat_spec, feat_spec, posm_spec, posm_spec],
        out_shape=[feat, feat, feat, feat, posm, posm],
        compiler_params=pltpu.CompilerParams(
            dimension_semantics=("parallel", "parallel"), vmem_limit_bytes=VMEM_LIMIT),
        name="proj_rope",
    )(x, g1, wqv, wk, cqT, sqT, ck, sk)


def _diff_attn_kernel(q_ref, k_ref, v_ref, lq1_ref, lk1_ref, lq2_ref, lk2_ref, g_ref, o_ref, s_buf, p_buf,
                      *, qb, nq):
    lam = (jnp.exp(jnp.sum(lq1_ref[...] * lk1_ref[...], axis=-1, keepdims=True))
           - jnp.exp(jnp.sum(lq2_ref[...] * lk2_ref[...], axis=-1, keepdims=True)) + LAM_INIT)
    gain = g_ref[...] * (1.0 - LAM_INIT)
    sub = qb // LANES
    row = lax.broadcasted_iota(jnp.int32, (PAIR, qb), 0)
    sub_head = (row % HEAD_DIM) // (DIFF_DIM // 2)
    kk = lax.broadcasted_iota(jnp.int32, (qb, qb), 0) // CHUNK
    qq = lax.broadcasted_iota(jnp.int32, (qb, qb), 1) // CHUNK
    diag_mask = jnp.where(kk <= qq, 0.0, NEG)
    ones = jnp.ones((ONES_ROWS, qb), BF16)

    NG = 4

    def make_w(i):
        qT = jnp.concatenate([q_ref[0, i * sub + t] for t in range(sub)], axis=1)
        zero = jnp.zeros_like(qT)
        return [jnp.where(sub_head == c, qT, zero) for c in range(NG)]

    def scores(w, blk, masked, slot, c):
        kblk = k_ref[0, blk * qb:(blk + 1) * qb, :]
        s = _dot(kblk, w[c])
        if masked:
            s = s + diag_mask
        s_buf[slot, :, c * qb:(c + 1) * qb] = s
        return jnp.max(s, axis=0, keepdims=True)

    def weights(m_blk, m, slot, c):
        m_new = m_blk if m is None else jnp.maximum(m, m_blk)
        p_buf[slot, :, c * qb:(c + 1) * qb] = jnp.exp2(s_buf[slot, :, c * qb:(c + 1) * qb] - m_new).astype(BF16)
        return m_new, (None if m is None else jnp.exp2(m - m_new))

    def accum(blk, alpha, acc, slot, c):
        hh = c // 2
        vx = jnp.concatenate(
            [jnp.concatenate([v_ref[0, blk * sub + t, hh * HEAD_DIM:(hh + 1) * HEAD_DIM, :]
                              for t in range(sub)], axis=1), ones], axis=0)
        d = _dot(vx, p_buf[slot, :, c * qb:(c + 1) * qb])
        return d if acc is None else alpha * acc + d

    def finalize(i, accs):
        ys = []
        for hh in range(2):
            o = []
            for c in range(2):
                a = accs[2 * hh + c]
                o.append(a[:HEAD_DIM] * (1.0 / a[HEAD_DIM:HEAD_DIM + 1]))
            d = o[0] - lam * o[1]
            ms = jnp.mean(d * d, axis=0, keepdims=True)
            ys.append(d * lax.rsqrt(ms + EPS) * gain)
        y = jnp.concatenate(ys, axis=0)
        o_ref[0, i * qb:(i + 1) * qb, :] = y.T.astype(BF16)

    ticks = [(i, blk) for i in range(nq) for blk in [i] + list(range(i))]
    last = {i: max(T for T, (ii, _) in enumerate(ticks) if ii == i) for i in range(nq)}
    st = {}
    for T in range(len(ticks) + 2):
        if T < len(ticks) and ticks[T][1] == ticks[T][0]:
            i = ticks[T][0]
            st[i] = dict(w=make_w(i), m=[None] * NG, alpha=[None] * NG, acc=[None] * NG, m_blk=[None] * NG)
        for c in range(NG):
            if T >= 2:
                i, blk = ticks[T - 2]
                S = st[i]
                S["acc"][c] = accum(blk, S["alpha"][c], S["acc"][c], T % 2, c)
            if 1 <= T <= len(ticks):
                i, blk = ticks[T - 1]
                S = st[i]
                S["m"][c], S["alpha"][c] = weights(S["m_blk"][c], S["m"][c], (T - 1) % 2, c)
            if T < len(ticks):
                i, blk = ticks[T]
                st[i]["m_blk"][c] = scores(st[i]["w"], blk, blk == i, T % 2, c)
        if T >= 2:
            i, blk = ticks[T - 2]
            if T - 2 == last[i]:
                finalize(i, st[i]["acc"])
                del st[i]


def _diff_attention(qT, k, vT, lq1, lk1, lq2, lk2, g, *, qb):
    B, S, _ = k.shape
    nblk = S // LANES
    feat_spec = pl.BlockSpec((1, nblk, PAIR, LANES), lambda b, p: (b, 0, p, 0))
    posm_spec = pl.BlockSpec((1, S, PAIR), lambda b, p: (b, 0, p))
    vec = pl.BlockSpec((1, DIFF_DIM), lambda b, p: (0, 0))
    return pl.pallas_call(
        functools.partial(_diff_attn_kernel, qb=qb, nq=S // qb),
        grid=(B, N_PAIRS),
        in_specs=[feat_spec, posm_spec, feat_spec, vec, vec, vec, vec,
                  pl.BlockSpec((HEAD_DIM, 1), lambda b, p: (0, 0))],
        out_specs=posm_spec,
        out_shape=jax.ShapeDtypeStruct((B, S, WIDTH_A), BF16),
        scratch_shapes=[pltpu.VMEM((2, qb, 4 * qb), F32), pltpu.VMEM((2, qb, 4 * qb), BF16)],
        compiler_params=pltpu.CompilerParams(
            dimension_semantics=("parallel", "parallel"), vmem_limit_bytes=VMEM_LIMIT),
        name="diff_attn",
    )(qT, k, vT, lq1, lk1, lq2, lk2, g)


BAND_BLOCKS = LEFT_CHUNKS * CHUNK // LANES + 1
BAND_KEYS = BAND_BLOCKS * LANES
BAND_PAIRS_PER_STEP = 2


def _band_attn_kernel(q_ref, k_ref, v_ref, bias_ref, o_ref, s_buf, p_buf, *, nq):
    row = lax.broadcasted_iota(jnp.int32, (PAIR, LANES), 0)
    rows_h = HEAD_DIM + ONES_ROWS

    GROUP = 2

    def span(i):
        nb = min(i + 1, BAND_BLOCKS)
        return i + 1 - nb, nb

    def groups(i):
        nb = span(i)[1]
        return [(g, min(g + GROUP, nb)) for g in range(0, nb, GROUP)]

    PP = BAND_PAIRS_PER_STEP

    def make_w(i, pp):
        qT = q_ref[0, i, pp * PAIR:(pp + 1) * PAIR, :]
        zero = jnp.zeros_like(qT)
        return jnp.concatenate([jnp.where(row < HEAD_DIM, qT, zero),
                                jnp.where(row >= HEAD_DIM, qT, zero)], axis=1)

    def scores(i, pp, w, slot, g0, g1):
        lo, nb = span(i)
        kblk = k_ref[0, (lo + g0) * LANES:(lo + g1) * LANES, pp * PAIR:(pp + 1) * PAIR]
        s = _dot(kblk, w) + bias_ref[pp, (BAND_BLOCKS - nb + g0) * LANES:(BAND_BLOCKS - nb + g1) * LANES, :]
        s_buf[slot, pp, g0 * LANES:g1 * LANES, :] = s
        return jnp.max(s, axis=0, keepdims=True)

    def weights(pp, m, slot, g0, g1):
        p_buf[slot, pp, g0 * LANES:g1 * LANES, :] = jnp.exp2(
            s_buf[slot, pp, g0 * LANES:g1 * LANES, :] - m).astype(BF16)

    def partial_out(i, pp, slot, g0, g1):
        lo, nb = span(i)
        ones = jnp.ones((ONES_ROWS, (g1 - g0) * LANES), BF16)
        vx = jnp.concatenate(
            [part for hh in range(2) for part in (
                jnp.concatenate([v_ref[0, lo + t, pp * PAIR + hh * HEAD_DIM:pp * PAIR + (hh + 1) * HEAD_DIM, :]
                                 for t in range(g0, g1)], axis=1), ones)], axis=0)
        return _dot(vx, p_buf[slot, pp, g0 * LANES:g1 * LANES, :])

    def store(i, pp, out):
        ys = []
        for hh in range(2):
            blk = out[hh * rows_h:(hh + 1) * rows_h, hh * LANES:(hh + 1) * LANES]
            ys.append(blk[:HEAD_DIM] * (1.0 / blk[HEAD_DIM:HEAD_DIM + 1]))
        y = jnp.concatenate(ys, axis=0)
        o_ref[0, i * LANES:(i + 1) * LANES, pp * PAIR:(pp + 1) * PAIR] = y.T.astype(BF16)

    m = {}
    for i in range(nq + 2):
        w = [make_w(i, pp) for pp in range(PP)] if i < nq else None
        out, mg = [None] * PP, [None] * PP
        ng = max(len(groups(j)) for j in (i - 2, i - 1, i) if 0 <= j < nq)
        for n in range(ng):
            for pp in range(PP):
                if i >= 2 and n < len(groups(i - 2)):
                    d = partial_out(i - 2, pp, i % 2, *groups(i - 2)[n])
                    out[pp] = d if out[pp] is None else out[pp] + d
                if 1 <= i <= nq and n < len(groups(i - 1)):
                    weights(pp, m[i - 1][pp], (i - 1) % 2, *groups(i - 1)[n])
                if i < nq and n < len(groups(i)):
                    mm = scores(i, pp, w[pp], i % 2, *groups(i)[n])
                    mg[pp] = mm if mg[pp] is None else jnp.maximum(mg[pp], mm)
        for pp in range(PP):
            if i >= 2:
                store(i - 2, pp, out[pp])
        if i < nq:
            m[i] = mg


def _band_attention(qT, k, vT, bias):
    B, S, _ = k.shape
    nblk = S // LANES
    PP = BAND_PAIRS_PER_STEP
    feat_spec = pl.BlockSpec((1, nblk, PP * PAIR, LANES), lambda b, p: (b, 0, p, 0))
    posm_spec = pl.BlockSpec((1, S, PP * PAIR), lambda b, p: (b, 0, p))
    return pl.pallas_call(
        functools.partial(_band_attn_kernel, nq=nblk),
        grid=(B, N_PAIRS // PP),
        in_specs=[feat_spec, posm_spec, feat_spec,
                  pl.BlockSpec((PP, BAND_KEYS, 2 * LANES), lambda b, p: (p, 0, 0))],
        out_specs=posm_spec,
        out_shape=jax.ShapeDtypeStruct((B, S, WIDTH_B), BF16),
        scratch_shapes=[pltpu.VMEM((2, PP, BAND_KEYS, 2 * LANES), F32),
                        pltpu.VMEM((2, PP, BAND_KEYS, 2 * LANES), BF16)],
        compiler_params=pltpu.CompilerParams(
            dimension_semantics=("parallel", "parallel"), vmem_limit_bytes=VMEM_LIMIT),
        name="band_attn",
    )(qT, k, vT, bias)


def _mlp_kernel(x_ref, oa_ref, ob_ref, woa_ref, wob_ref, g2_ref, w1_ref, w2_ref, gf_ref, out_ref, *, ff_chunk):
    y = x_ref[...] + _dot(oa_ref[...], woa_ref[...]) + _dot(ob_ref[...], wob_ref[...])
    ms = jnp.mean(y * y, axis=-1, keepdims=True)
    h = ((y * lax.rsqrt(ms + EPS)) * g2_ref[...]).astype(BF16)
    mlp = None
    for c in range(D_FF // ff_chunk):
        u = _dot(h, w1_ref[:, c * ff_chunk:(c + 1) * ff_chunk])
        u = jnp.maximum(u, 0.0)
        d = _dot((u * u).astype(BF16), w2_ref[c * ff_chunk:(c + 1) * ff_chunk, :])
        mlp = d if mlp is None else mlp + d
    acc = y + mlp
    ms = jnp.mean(acc * acc, axis=-1, keepdims=True)
    out_ref[...] = (acc * lax.rsqrt(ms + EPS)) * gf_ref[...]


def _out_mlp(x2d, oa, ob, woa, wob, g2, w1, w2, gf, *, tm, ff_chunk):
    T = x2d.shape[0]
    const = lambda shape: pl.BlockSpec(shape, lambda i: (0,) * len(shape), pipeline_mode=pl.Buffered(1))
    return pl.pallas_call(
        functools.partial(_mlp_kernel, ff_chunk=ff_chunk),
        grid=(T // tm,),
        in_specs=[
            pl.BlockSpec((tm, D_MODEL), lambda i: (i, 0)),
            pl.BlockSpec((tm, WIDTH_A), lambda i: (i, 0)),
            pl.BlockSpec((tm, WIDTH_B), lambda i: (i, 0)),
            const(woa.shape), const(wob.shape), const((1, D_MODEL)),
            const(w1.shape), const(w2.shape), const((1, D_MODEL)),
        ],
        out_specs=pl.BlockSpec((tm, D_MODEL), lambda i: (i, 0)),
        out_shape=jax.ShapeDtypeStruct((T, D_MODEL), F32),
        compiler_params=pltpu.CompilerParams(
            dimension_semantics=("parallel",), vmem_limit_bytes=VMEM_LIMIT),
        name="out_mlp",
    )(x2d, oa, ob, woa, wob, g2, w1, w2, gf)


def _pair_layout(wcols):
    half = DIFF_DIM // 2
    return (wcols.reshape(-1, N_PAIRS, PAIR // DIFF_DIM, 2, half).transpose(0, 1, 3, 2, 4)
            .reshape(-1, WIDTH_A))


def _rope_tables(S):
    half = DIFF_DIM // 2
    inv_freq = ROPE_THETA ** (-jnp.arange(half, dtype=F32) / half)
    ang = jnp.arange(S).astype(F32)[:, None] * inv_freq[None, :]
    cos, sin = jnp.cos(ang), jnp.sin(ang)
    scale = DIFF_DIM ** -0.5 * LOG2E
    cqT = jnp.tile(cos.T, (HEAD_DIM // half, 1)) * scale
    sqT = jnp.tile(sin.T, (HEAD_DIM // half, 1)) * scale
    ck = jnp.tile(cos, (1, PAIR // half))
    sign = jnp.where(jnp.arange(PAIR) < HEAD_DIM, -1.0, 1.0).astype(F32)
    sk = jnp.tile(sin, (1, PAIR // half)) * sign[None, :]
    return cqT, sqT, ck, sk


def _band_bias(rel_table):
    H = rel_table.shape[0]
    span, width = BAND_KEYS, 2 * BAND_KEYS
    back = LEFT_CHUNKS * CHUNK
    tbl = rel_table.astype(F32) * LOG2E
    left = span - MAX_REL_DIST
    right = width + 1 - left - tbl.shape[1]
    ext = jnp.concatenate([jnp.broadcast_to(tbl[:, :1], (H, left)), tbl,
                           jnp.broadcast_to(tbl[:, -1:], (H, right))], axis=1)
    vals = jnp.stack([ext[:, span + back - j:span + back - j + LANES] for j in range(span)], axis=1)
    j = jnp.arange(span)[:, None]
    q = back + jnp.arange(LANES)[None, :]
    dc = q // CHUNK - j // CHUNK
    ok = (dc >= 0) & (dc <= LEFT_CHUNKS)
    bias = jnp.where(ok[None], vals, NEG)
    return bias.reshape(N_PAIRS, 2, span, LANES).transpose(0, 2, 1, 3).reshape(N_PAIRS, span, 2 * LANES)


def kernel(x, w_in, w_out, norm1_g, norm2_g, final_g, subln_g, lambda_q1, lambda_k1, lambda_q2, lambda_k2,
           rel_bias, w_ff1, w_ff2):
    B, S, D = x.shape
    depth = w_in.shape[0]
    assert D == D_MODEL and depth == 1 and S % PROJ_ROWS == 0 and (B * S) % MLP_ROWS == 0
    l = 0
    w = w_in[l]
    c0, c1, c2, c3, c4 = np.cumsum([WIDTH_A, WIDTH_A, WIDTH_A, WIDTH_B, WIDTH_B]).tolist()
    wq_a = _pair_layout(w[:, :c0])
    wk_a = _pair_layout(w[:, c0:c1])
    wqv = jnp.concatenate([wq_a, w[:, c1:c2], w[:, c2:c3], w[:, c4:]], axis=1).T.astype(BF16)
    wk = jnp.concatenate([wk_a, w[:, c3:c4]], axis=1).astype(BF16)
    cqT, sqT, ck, sk = _rope_tables(S)

    qaT, vaT, qbT, vbT, ka, kb = _project(
        x, norm1_g[l].reshape(1, D), wqv, wk, cqT, sqT, ck, sk, tm=PROJ_ROWS)

    oa = _diff_attention(
        qaT, ka, vaT,
        lambda_q1[l].reshape(1, -1).astype(F32), lambda_k1[l].reshape(1, -1).astype(F32),
        lambda_q2[l].reshape(1, -1).astype(F32), lambda_k2[l].reshape(1, -1).astype(F32),
        subln_g[l].reshape(HEAD_DIM, 1), qb=DIFF_BLOCK)
    ob = _band_attention(qbT, kb, vbT, _band_bias(rel_bias[l]))

    wo = w_out[l].astype(BF16)
    out = _out_mlp(
        x.reshape(B * S, D), oa.reshape(B * S, WIDTH_A), ob.reshape(B * S, WIDTH_B),
        wo[:WIDTH_A], wo[WIDTH_A:], norm2_g[l].reshape(1, D),
        w_ff1[l].astype(BF16), w_ff2[l].astype(BF16), final_g.reshape(1, D),
        tm=MLP_ROWS, ff_chunk=MLP_FF_CHUNK)
    return out.reshape(B, S, D)
```

```python
import functools
import math

import jax
import jax.numpy as jnp
import numpy as np
from jax import lax
from jax.experimental import pallas as pl
from jax.experimental.pallas import tpu as pltpu

D_MODEL = 1024
HEAD_DIM = 64
DIFF_DIM = 32
N_HEADS_A = 8
N_HEADS_B = 8
WIDTH_A = N_HEADS_A * HEAD_DIM
WIDTH_B = N_HEADS_B * HEAD_DIM
CHUNK = 64
LEFT_CHUNKS = 8
MAX_REL_DIST = 256
D_FF = 4 * D_MODEL
ROPE_THETA = 10000.0
EPS = 1e-6
LAM_INIT = 0.8 - 0.6 * math.exp(-0.3 * 0)

LANES = 128
PAIR = 2 * HEAD_DIM
N_PAIRS = WIDTH_A // PAIR
VMEM_LIMIT = 56 * 1024 * 1024

PROJ_ROWS = 1024
MLP_ROWS = 1024
MLP_PIECE_ROWS = 512
MLP_PIECE_SKEW = 3
MLP_FF_CHUNK = 1024
DIFF_BLOCK = 256

BF16 = jnp.bfloat16
F32 = jnp.float32
NEG = -1e30
ONES_ROWS = 16
LOG2E = math.log2(math.e)


def _dot(a, b):
    return jnp.dot(a, b, preferred_element_type=F32)


def _dot_nt(a, b):
    return lax.dot_general(a, b, (((1,), (1,)), ((), ())), preferred_element_type=F32)


def _proj_kernel(x_ref, g_ref, wqv_ref, wk_ref, cqT_ref, sqT_ref, ck_ref, sk_ref,
                 qa_ref, va_ref, qb_ref, vb_ref, ka_ref, kb_ref, *, tm):
    x = x_ref[0]
    ms = jnp.mean(x * x, axis=-1, keepdims=True)
    h = ((x * lax.rsqrt(ms + EPS)) * g_ref[...]).astype(BF16)
    nblk = tm // LANES

    qvT = _dot_nt(wqv_ref[...], h)
    cq = cqT_ref[...]
    sq = sqT_ref[...]
    for g in range(N_PAIRS):
        x1 = qvT[g * PAIR:g * PAIR + HEAD_DIM]
        x2 = qvT[g * PAIR + HEAD_DIM:(g + 1) * PAIR]
        r1 = (x1 * cq - x2 * sq).astype(BF16)
        r2 = (x2 * cq + x1 * sq).astype(BF16)
        for t in range(nblk):
            qa_ref[0, t, g * PAIR:g * PAIR + HEAD_DIM, :] = r1[:, t * LANES:(t + 1) * LANES]
            qa_ref[0, t, g * PAIR + HEAD_DIM:(g + 1) * PAIR, :] = r2[:, t * LANES:(t + 1) * LANES]
    va = qvT[WIDTH_A:2 * WIDTH_A].astype(BF16)
    qb = (qvT[2 * WIDTH_A:2 * WIDTH_A + WIDTH_B] * (HEAD_DIM ** -0.5 * LOG2E)).astype(BF16)
    vb = qvT[2 * WIDTH_A + WIDTH_B:].astype(BF16)
    for t in range(nblk):
        va_ref[0, t] = va[:, t * LANES:(t + 1) * LANES]
        qb_ref[0, t] = qb[:, t * LANES:(t + 1) * LANES]
        vb_ref[0, t] = vb[:, t * LANES:(t + 1) * LANES]

    k = _dot(h, wk_ref[...])
    ck = ck_ref[...]
    sk = sk_ref[...]
    for g in range(N_PAIRS):
        xk = k[:, g * PAIR:(g + 1) * PAIR]
        ka_ref[0, g] = (xk * ck + pltpu.roll(xk, HEAD_DIM, 1) * sk).astype(BF16)
        kb_ref[0, g] = k[:, WIDTH_A + g * PAIR:WIDTH_A + (g + 1) * PAIR].astype(BF16)


def _project(x, g1, wqv, wk, cqT, sqT, ck, sk, *, tm):
    B, S, _ = x.shape
    nblk = tm // LANES
    feat = jax.ShapeDtypeStruct((B, S // LANES, WIDTH_A, LANES), BF16)
    posm = jax.ShapeDtypeStruct((B, N_PAIRS, S, PAIR), BF16)
    feat_spec = pl.BlockSpec((1, nblk, WIDTH_A, LANES), lambda b, i: (b, i, 0, 0))
    posm_spec = pl.BlockSpec((1, N_PAIRS, tm, PAIR), lambda b, i: (b, 0, i, 0))
    const = lambda shape: pl.BlockSpec(shape, lambda b, i: (0,) * len(shape), pipeline_mode=pl.Buffered(1))
    return pl.pallas_call(
        functools.partial(_proj_kernel, tm=tm),
        grid=(B, S // tm),
        in_specs=[
            pl.BlockSpec((1, tm, D_MODEL), lambda b, i: (b, i, 0)),
            const((1, D_MODEL)),
            const(wqv.shape),
            const(wk.shape),
            pl.BlockSpec((HEAD_DIM, tm), lambda b, i: (0, i)),
            pl.BlockSpec((HEAD_DIM, tm), lambda b, i: (0, i)),
            pl.BlockSpec((tm, PAIR), lambda b, i: (i, 0)),
            pl.BlockSpec((tm, PAIR), lambda b, i: (i, 0)),
        ],
        out_specs=[feat_spec, feat_spec, feat_spec, feat_spec, posm_spec, posm_spec],
        out_shape=[feat, feat, feat, feat, posm, posm],
        compiler_params=pltpu.CompilerParams(
            dimension_semantics=("parallel", "parallel"), vmem_limit_bytes=VMEM_LIMIT),
        name="proj_rope",
    )(x, g1, wqv, wk, cqT, sqT, ck, sk)


def _diff_attn_kernel(q_ref, k_ref, v_ref, lq1_ref, lk1_ref, lq2_ref, lk2_ref, g_ref, o_ref, s_buf, p_buf,
                      *, qb, nq):
    lam = (jnp.exp(jnp.sum(lq1_ref[...] * lk1_ref[...], axis=-1, keepdims=True))
           - jnp.exp(jnp.sum(lq2_ref[...] * lk2_ref[...], axis=-1, keepdims=True)) + LAM_INIT)
    gain = g_ref[...] * (1.0 - LAM_INIT)
    sub = qb // LANES
    row = lax.broadcasted_iota(jnp.int32, (PAIR, qb), 0)
    sub_head = (row % HEAD_DIM) // (DIFF_DIM // 2)
    kk = lax.broadcasted_iota(jnp.int32, (qb, qb), 0) // CHUNK
    qq = lax.broadcasted_iota(jnp.int32, (qb, qb), 1) // CHUNK
    diag_mask = jnp.where(kk <= qq, 0.0, NEG)
    ones = jnp.ones((ONES_ROWS, qb), BF16)

    NG = 4

    def make_w(i):
        qT = jnp.concatenate([q_ref[0, i * sub + t] for t in range(sub)], axis=1)
        zero = jnp.zeros_like(qT)
        return [jnp.where(sub_head == c, qT, zero) for c in range(NG)]

    def scores(w, blk, masked, slot, c):
        kblk = k_ref[0, 0, blk * qb:(blk + 1) * qb, :]
        s = _dot(kblk, w[c])
        if masked:
            s = s + diag_mask
        s_buf[slot, :, c * qb:(c + 1) * qb] = s
        return jnp.max(s, axis=0, keepdims=True)

    def weights(m_blk, m, slot, c):
        m_new = m_blk if m is None else jnp.maximum(m, m_blk)
        p_buf[slot, :, c * qb:(c + 1) * qb] = jnp.exp2(s_buf[slot, :, c * qb:(c + 1) * qb] - m_new).astype(BF16)
        return m_new, (None if m is None else jnp.exp2(m - m_new))

    def accum(blk, alpha, acc, slot, c):
        hh = c // 2
        vx = jnp.concatenate(
            [jnp.concatenate([v_ref[0, blk * sub + t, hh * HEAD_DIM:(hh + 1) * HEAD_DIM, :]
                              for t in range(sub)], axis=1), ones], axis=0)
        d = _dot(vx, p_buf[slot, :, c * qb:(c + 1) * qb])
        return d if acc is None else alpha * acc + d

    def finalize(i, accs):
        ys = []
        for hh in range(2):
            o = []
            for c in range(2):
                a = accs[2 * hh + c]
                o.append(a[:HEAD_DIM] * (1.0 / a[HEAD_DIM:HEAD_DIM + 1]))
            d = o[0] - lam * o[1]
            ms = jnp.mean(d * d, axis=0, keepdims=True)
            ys.append(d * lax.rsqrt(ms + EPS) * gain)
        y = jnp.concatenate(ys, axis=0)
        o_ref[0, 0, i * qb:(i + 1) * qb, :] = y.T.astype(BF16)

    ticks = [(i, blk) for i in range(nq) for blk in [i] + list(range(i))]
    last = {i: max(T for T, (ii, _) in enumerate(ticks) if ii == i) for i in range(nq)}
    st = {}
    for T in range(len(ticks) + 2):
        if T < len(ticks) and ticks[T][1] == ticks[T][0]:
            i = ticks[T][0]
            st[i] = dict(w=make_w(i), m=[None] * NG, alpha=[None] * NG, acc=[None] * NG, m_blk=[None] * NG)
        for c in range(NG):
            if T >= 2:
                i, blk = ticks[T - 2]
                S = st[i]
                S["acc"][c] = accum(blk, S["alpha"][c], S["acc"][c], T % 2, c)
            if 1 <= T <= len(ticks):
                i, blk = ticks[T - 1]
                S = st[i]
                S["m"][c], S["alpha"][c] = weights(S["m_blk"][c], S["m"][c], (T - 1) % 2, c)
            if T < len(ticks):
                i, blk = ticks[T]
                st[i]["m_blk"][c] = scores(st[i]["w"], blk, blk == i, T % 2, c)
        if T >= 2:
            i, blk = ticks[T - 2]
            if T - 2 == last[i]:
                finalize(i, st[i]["acc"])
                del st[i]


def _diff_attention(qT, k, vT, lq1, lk1, lq2, lk2, g, *, qb):
    B, _, S, _ = k.shape
    nblk = S // LANES
    feat_spec = pl.BlockSpec((1, nblk, PAIR, LANES), lambda b, p: (b, 0, p, 0))
    key_spec = pl.BlockSpec((1, 1, S, PAIR), lambda b, p: (b, p, 0, 0))
    vec = pl.BlockSpec((1, DIFF_DIM), lambda b, p: (0, 0))
    return pl.pallas_call(
        functools.partial(_diff_attn_kernel, qb=qb, nq=S // qb),
        grid=(B, N_PAIRS),
        in_specs=[feat_spec, key_spec, feat_spec, vec, vec, vec, vec,
                  pl.BlockSpec((HEAD_DIM, 1), lambda b, p: (0, 0))],
        out_specs=key_spec,
        out_shape=jax.ShapeDtypeStruct((B, N_PAIRS, S, PAIR), BF16),
        scratch_shapes=[pltpu.VMEM((2, qb, 4 * qb), F32), pltpu.VMEM((2, qb, 4 * qb), BF16)],
        compiler_params=pltpu.CompilerParams(
            dimension_semantics=("parallel", "parallel"), vmem_limit_bytes=VMEM_LIMIT),
        name="diff_attn",
    )(qT, k, vT, lq1, lk1, lq2, lk2, g)


BAND_BLOCKS = LEFT_CHUNKS * CHUNK // LANES + 1
BAND_KEYS = BAND_BLOCKS * LANES
BAND_PAIRS_PER_STEP = 4


def _band_attn_kernel(q_ref, k_ref, v_ref, bias_ref, o_ref, s_buf, p_buf, *, nq):
    row = lax.broadcasted_iota(jnp.int32, (PAIR, LANES), 0)
    rows_h = HEAD_DIM + ONES_ROWS

    GROUP = 2

    def span(i):
        nb = min(i + 1, BAND_BLOCKS)
        return i + 1 - nb, nb

    def groups(i):
        nb = span(i)[1]
        return [(g, min(g + GROUP, nb)) for g in range(0, nb, GROUP)]

    PP = BAND_PAIRS_PER_STEP

    def make_w(i, pp):
        qT = q_ref[0, i, pp * PAIR:(pp + 1) * PAIR, :]
        zero = jnp.zeros_like(qT)
        return jnp.concatenate([jnp.where(row < HEAD_DIM, qT, zero),
                                jnp.where(row >= HEAD_DIM, qT, zero)], axis=1)

    def scores(i, pp, w, slot, g0, g1):
        lo, nb = span(i)
        kblk = k_ref[0, pp, (lo + g0) * LANES:(lo + g1) * LANES, :]
        s = _dot(kblk, w) + bias_ref[pp, (BAND_BLOCKS - nb + g0) * LANES:(BAND_BLOCKS - nb + g1) * LANES, :]
        s_buf[slot, pp, g0 * LANES:g1 * LANES, :] = s
        return jnp.max(s, axis=0, keepdims=True)

    def weights(pp, m, slot, g0, g1):
        p_buf[slot, pp, g0 * LANES:g1 * LANES, :] = jnp.exp2(
            s_buf[slot, pp, g0 * LANES:g1 * LANES, :] - m).astype(BF16)

    def partial_out(i, pp, slot, g0, g1):
        lo, nb = span(i)
        ones = jnp.ones((ONES_ROWS, (g1 - g0) * LANES), BF16)
        vx = jnp.concatenate(
            [part for hh in range(2) for part in (
                jnp.concatenate([v_ref[0, lo + t, pp * PAIR + hh * HEAD_DIM:pp * PAIR + (hh + 1) * HEAD_DIM, :]
                                 for t in range(g0, g1)], axis=1), ones)], axis=0)
        return _dot(vx, p_buf[slot, pp, g0 * LANES:g1 * LANES, :])

    def store(i, pp, out):
        ys = []
        for hh in range(2):
            blk = out[hh * rows_h:(hh + 1) * rows_h, hh * LANES:(hh + 1) * LANES]
            ys.append(blk[:HEAD_DIM] * (1.0 / blk[HEAD_DIM:HEAD_DIM + 1]))
        y = jnp.concatenate(ys, axis=0)
        o_ref[0, pp, i * LANES:(i + 1) * LANES, :] = y.T.astype(BF16)

    m = {}
    for i in range(nq + 2):
        w = [make_w(i, pp) for pp in range(PP)] if i < nq else None
        out, mg = [None] * PP, [None] * PP
        ng = max(len(groups(j)) for j in (i - 2, i - 1, i) if 0 <= j < nq)
        for n in range(ng):
            for pp in range(PP):
                if i >= 2 and n < len(groups(i - 2)):
                    d = partial_out(i - 2, pp, i % 2, *groups(i - 2)[n])
                    out[pp] = d if out[pp] is None else out[pp] + d
                if 1 <= i <= nq and n < len(groups(i - 1)):
                    weights(pp, m[i - 1][pp], (i - 1) % 2, *groups(i - 1)[n])
                if i < nq and n < len(groups(i)):
                    mm = scores(i, pp, w[pp], i % 2, *groups(i)[n])
                    mg[pp] = mm if mg[pp] is None else jnp.maximum(mg[pp], mm)
        for pp in range(PP):
            if i >= 2:
                store(i - 2, pp, out[pp])
        if i < nq:
            m[i] = mg


def _band_attention(qT, k, vT, bias):
    B, _, S, _ = k.shape
    nblk = S // LANES
    PP = BAND_PAIRS_PER_STEP
    feat_spec = pl.BlockSpec((1, nblk, PP * PAIR, LANES), lambda p, b: (b, 0, p, 0))
    key_spec = pl.BlockSpec((1, PP, S, PAIR), lambda p, b: (b, p, 0, 0))
    return pl.pallas_call(
        functools.partial(_band_attn_kernel, nq=nblk),
        grid=(N_PAIRS // PP, B),
        in_specs=[feat_spec, key_spec, feat_spec,
                  pl.BlockSpec((PP, BAND_KEYS, 2 * LANES), lambda p, b: (p, 0, 0))],
        out_specs=key_spec,
        out_shape=jax.ShapeDtypeStruct((B, N_PAIRS, S, PAIR), BF16),
        scratch_shapes=[pltpu.VMEM((2, PP, BAND_KEYS, 2 * LANES), F32),
                        pltpu.VMEM((2, PP, BAND_KEYS, 2 * LANES), BF16)],
        compiler_params=pltpu.CompilerParams(
            dimension_semantics=("parallel", "parallel"), vmem_limit_bytes=VMEM_LIMIT),
        name="band_attn",
    )(qT, k, vT, bias)


def _mlp_kernel(x_ref, oa_ref, ob_ref, woa_ref, wob_ref, g2_ref, w1_ref, w2_ref, gf_ref, out_ref,
                *, ff_chunk, sub):
    n_chunk = D_FF // ff_chunk
    pieces = x_ref.shape[0] // sub

    def stage_in(r):
        rows = slice(r * sub, (r + 1) * sub)
        oa = jnp.concatenate([oa_ref[0, g, rows, :] for g in range(N_PAIRS)], axis=1)
        ob = jnp.concatenate([ob_ref[0, g, rows, :] for g in range(N_PAIRS)], axis=1)
        return x_ref[rows, :] + _dot(oa, woa_ref[...]) + _dot(ob, wob_ref[...])

    def norm(y):
        ms = jnp.mean(y * y, axis=-1, keepdims=True)
        return ((y * lax.rsqrt(ms + EPS)) * g2_ref[...]).astype(BF16)

    def up(h, c):
        return _dot(h, w1_ref[:, c * ff_chunk:(c + 1) * ff_chunk])

    def down(u, c):
        u = jnp.maximum(u, 0.0)
        return _dot((u * u).astype(BF16), w2_ref[c * ff_chunk:(c + 1) * ff_chunk, :])

    def stage_out(r, y, mlp):
        acc = y + mlp
        ms = jnp.mean(acc * acc, axis=-1, keepdims=True)
        out_ref[r * sub:(r + 1) * sub, :] = (acc * lax.rsqrt(ms + EPS)) * gf_ref[...]

    st = [dict() for _ in range(pieces)]

    def event(r, e):
        S = st[r]
        if e == 0:
            S["y"] = stage_in(r)
        elif e == 1:
            S["h"] = norm(S["y"])
            S["u"] = up(S["h"], 0)
            S["mlp"] = None
        elif e < 2 + n_chunk:
            c = e - 2
            u_next = up(S["h"], c + 1) if c + 1 < n_chunk else None
            d = down(S["u"], c)
            S["mlp"] = d if S["mlp"] is None else S["mlp"] + d
            S["u"] = u_next
        else:
            stage_out(r, S["y"], S["mlp"])

    n_event = 3 + n_chunk
    for t in range(n_event + MLP_PIECE_SKEW * (pieces - 1)):
        for r in range(pieces):
            e = t - MLP_PIECE_SKEW * r
            if 0 <= e < n_event:
                event(r, e)


def _out_mlp(x2d, oa, ob, woa, wob, g2, w1, w2, gf, *, tm, ff_chunk):
    T = x2d.shape[0]
    tiles_per_seq = oa.shape[2] // tm
    mix_spec = pl.BlockSpec((1, N_PAIRS, tm, PAIR), lambda i: (i // tiles_per_seq, 0, i % tiles_per_seq, 0))
    const = lambda shape: pl.BlockSpec(shape, lambda i: (0,) * len(shape), pipeline_mode=pl.Buffered(1))
    return pl.pallas_call(
        functools.partial(_mlp_kernel, ff_chunk=ff_chunk, sub=MLP_PIECE_ROWS),
        grid=(T // tm,),
        in_specs=[
            pl.BlockSpec((tm, D_MODEL), lambda i: (i, 0)),
            mix_spec, mix_spec,
            const(woa.shape), const(wob.shape), const((1, D_MODEL)),
            const(w1.shape), const(w2.shape), const((1, D_MODEL)),
        ],
        out_specs=pl.BlockSpec((tm, D_MODEL), lambda i: (i, 0)),
        out_shape=jax.ShapeDtypeStruct((T, D_MODEL), F32),
        compiler_params=pltpu.CompilerParams(
            dimension_semantics=("parallel",), vmem_limit_bytes=VMEM_LIMIT),
        name="out_mlp",
    )(x2d, oa, ob, woa, wob, g2, w1, w2, gf)


def _pair_layout(wcols):
    half = DIFF_DIM // 2
    return (wcols.reshape(-1, N_PAIRS, PAIR // DIFF_DIM, 2, half).transpose(0, 1, 3, 2, 4)
            .reshape(-1, WIDTH_A))


def _rope_tables(S):
    half = DIFF_DIM // 2
    inv_freq = ROPE_THETA ** (-jnp.arange(half, dtype=F32) / half)
    ang = jnp.arange(S).astype(F32)[:, None] * inv_freq[None, :]
    cos, sin = jnp.cos(ang), jnp.sin(ang)
    scale = DIFF_DIM ** -0.5 * LOG2E
    cqT = jnp.tile(cos.T, (HEAD_DIM // half, 1)) * scale
    sqT = jnp.tile(sin.T, (HEAD_DIM // half, 1)) * scale
    ck = jnp.tile(cos, (1, PAIR // half))
    sign = jnp.where(jnp.arange(PAIR) < HEAD_DIM, -1.0, 1.0).astype(F32)
    sk = jnp.tile(sin, (1, PAIR // half)) * sign[None, :]
    return cqT, sqT, ck, sk


def _band_bias(rel_table):
    H = rel_table.shape[0]
    span, width = BAND_KEYS, 2 * BAND_KEYS
    back = LEFT_CHUNKS * CHUNK
    tbl = rel_table.astype(F32) * LOG2E
    left = span - MAX_REL_DIST
    right = width + 1 - left - tbl.shape[1]
    ext = jnp.concatenate([jnp.broadcast_to(tbl[:, :1], (H, left)), tbl,
                           jnp.broadcast_to(tbl[:, -1:], (H, right))], axis=1)
    wide = jnp.stack([ext[:, span - r:span - r + span] for r in range(LANES)], axis=1)
    vals = wide.reshape(H, LANES, BAND_BLOCKS, LANES)[:, :, ::-1, :].transpose(0, 2, 1, 3).reshape(H, span, LANES)
    j = jnp.arange(span)[:, None]
    q = back + jnp.arange(LANES)[None, :]
    dc = q // CHUNK - j // CHUNK
    ok = (dc >= 0) & (dc <= LEFT_CHUNKS)
    bias = jnp.where(ok[None], vals, NEG)
    return bias.reshape(N_PAIRS, 2, span, LANES).transpose(0, 2, 1, 3).reshape(N_PAIRS, span, 2 * LANES)


def kernel(x, w_in, w_out, norm1_g, norm2_g, final_g, subln_g, lambda_q1, lambda_k1, lambda_q2, lambda_k2,
           rel_bias, w_ff1, w_ff2):
    B, S, D = x.shape
    depth = w_in.shape[0]
    assert D == D_MODEL and depth == 1 and S % PROJ_ROWS == 0 and S % MLP_ROWS == 0
    l = 0
    w = w_in[l]
    c0, c1, c2, c3, c4 = np.cumsum([WIDTH_A, WIDTH_A, WIDTH_A, WIDTH_B, WIDTH_B]).tolist()
    wq_a = _pair_layout(w[:, :c0])
    wk_a = _pair_layout(w[:, c0:c1])
    wqv = jnp.concatenate([wq_a, w[:, c1:c2], w[:, c2:c3], w[:, c4:]], axis=1).T.astype(BF16)
    wk = jnp.concatenate([wk_a, w[:, c3:c4]], axis=1).astype(BF16)
    cqT, sqT, ck, sk = _rope_tables(S)

    qaT, vaT, qbT, vbT, ka, kb = _project(
        x, norm1_g[l].reshape(1, D), wqv, wk, cqT, sqT, ck, sk, tm=PROJ_ROWS)

    oa = _diff_attention(
        qaT, ka, vaT,
        lambda_q1[l].reshape(1, -1).astype(F32), lambda_k1[l].reshape(1, -1).astype(F32),
        lambda_q2[l].reshape(1, -1).astype(F32), lambda_k2[l].reshape(1, -1).astype(F32),
        subln_g[l].reshape(HEAD_DIM, 1), qb=DIFF_BLOCK)
    ob = _band_attention(qbT, kb, vbT, _band_bias(rel_bias[l]))

    wo = w_out[l].astype(BF16)
    out = _out_mlp(
        x.reshape(B * S, D), oa, ob,
        wo[:WIDTH_A], wo[WIDTH_A:], norm2_g[l].reshape(1, D),
        w_ff1[l].astype(BF16), w_ff2[l].astype(BF16), final_g.reshape(1, D),
        tm=MLP_ROWS, ff_chunk=MLP_FF_CHUNK)
    return out.reshape(B, S, D)
```

```python
import functools
import math

import jax
import jax.numpy as jnp
import numpy as np
from jax import lax
from jax.experimental import pallas as pl
from jax.experimental.pallas import tpu as pltpu

D_MODEL = 1024
HEAD_DIM = 64
DIFF_DIM = 32
N_HEADS_A = 8
N_HEADS_B = 8
WIDTH_A = N_HEADS_A * HEAD_DIM
WIDTH_B = N_HEADS_B * HEAD_DIM
CHUNK = 64
LEFT_CHUNKS = 8
MAX_REL_DIST = 256
D_FF = 4 * D_MODEL
ROPE_THETA = 10000.0
EPS = 1e-6
LAM_INIT = 0.8 - 0.6 * math.exp(-0.3 * 0)

LANES = 128
PAIR = 2 * HEAD_DIM
N_PAIRS = WIDTH_A // PAIR
VMEM_LIMIT = 56 * 1024 * 1024

PROJ_ROWS = 1024
MLP_ROWS = 1024
MLP_PIECE_ROWS = 512
MLP_PIECE_SKEW = 3
MLP_FF_CHUNK = 1024
DIFF_BLOCK = 256

BF16 = jnp.bfloat16
F32 = jnp.float32
NEG = -1e30
ONES_ROWS = 16
LOG2E = math.log2(math.e)


def _dot(a, b):
    return jnp.dot(a, b, preferred_element_type=F32)


def _dot_nt(a, b):
    return lax.dot_general(a, b, (((1,), (1,)), ((), ())), preferred_element_type=F32)


def _proj_kernel(x_ref, g_ref, wqv_ref, wk_ref, cqT_ref, sqT_ref, ck_ref, sk_ref,
                 qa_ref, va_ref, qb_ref, vb_ref, ka_ref, kb_ref, *, tm):
    x = x_ref[0]
    ms = jnp.mean(x * x, axis=-1, keepdims=True)
    h = ((x * lax.rsqrt(ms + EPS)) * g_ref[...]).astype(BF16)
    nblk = tm // LANES

    qvT = _dot_nt(wqv_ref[...], h)
    cq = cqT_ref[...]
    sq = sqT_ref[...]
    for g in range(N_PAIRS):
        x1 = qvT[g * PAIR:g * PAIR + HEAD_DIM]
        x2 = qvT[g * PAIR + HEAD_DIM:(g + 1) * PAIR]
        r1 = (x1 * cq - x2 * sq).astype(BF16)
        r2 = (x2 * cq + x1 * sq).astype(BF16)
        for t in range(nblk):
            qa_ref[0, t, g * PAIR:g * PAIR + HEAD_DIM, :] = r1[:, t * LANES:(t + 1) * LANES]
            qa_ref[0, t, g * PAIR + HEAD_DIM:(g + 1) * PAIR, :] = r2[:, t * LANES:(t + 1) * LANES]
    va = qvT[WIDTH_A:2 * WIDTH_A].astype(BF16)
    qb = (qvT[2 * WIDTH_A:2 * WIDTH_A + WIDTH_B] * (HEAD_DIM ** -0.5 * LOG2E)).astype(BF16)
    vb = qvT[2 * WIDTH_A + WIDTH_B:].astype(BF16)
    for t in range(nblk):
        va_ref[0, t] = va[:, t * LANES:(t + 1) * LANES]
        qb_ref[0, t] = qb[:, t * LANES:(t + 1) * LANES]
        vb_ref[0, t] = vb[:, t * LANES:(t + 1) * LANES]

    k = _dot(h, wk_ref[...])
    ck = ck_ref[...]
    sk = sk_ref[...]
    for g in range(N_PAIRS):
        xk = k[:, g * PAIR:(g + 1) * PAIR]
        ka_ref[0, g] = (xk * ck + pltpu.roll(xk, HEAD_DIM, 1) * sk).astype(BF16)
        kb_ref[0, g] = k[:, WIDTH_A + g * PAIR:WIDTH_A + (g + 1) * PAIR].astype(BF16)


def _project(x, g1, wqv, wk, cqT, sqT, ck, sk, *, tm):
    B, S, _ = x.shape
    nblk = tm // LANES
    feat = jax.ShapeDtypeStruct((B, S // LANES, WIDTH_A, LANES), BF16)
    posm = jax.ShapeDtypeStruct((B, N_PAIRS, S, PAIR), BF16)
    feat_spec = pl.BlockSpec((1, nblk, WIDTH_A, LANES), lambda b, i: (b, i, 0, 0))
    posm_spec = pl.BlockSpec((1, N_PAIRS, tm, PAIR), lambda b, i: (b, 0, i, 0))
    const = lambda shape: pl.BlockSpec(shape, lambda b, i: (0,) * len(shape), pipeline_mode=pl.Buffered(1))
    return pl.pallas_call(
        functools.partial(_proj_kernel, tm=tm),
        grid=(B, S // tm),
        in_specs=[
            pl.BlockSpec((1, tm, D_MODEL), lambda b, i: (b, i, 0)),
            const((1, D_MODEL)),
            const(wqv.shape),
            const(wk.shape),
            pl.BlockSpec((HEAD_DIM, tm), lambda b, i: (0, i)),
            pl.BlockSpec((HEAD_DIM, tm), lambda b, i: (0, i)),
            pl.BlockSpec((tm, PAIR), lambda b, i: (i, 0)),
            pl.BlockSpec((tm, PAIR), lambda b, i: (i, 0)),
        ],
        out_specs=[feat_spec, feat_spec, feat_spec, feat_spec, posm_spec, posm_spec],
        out_shape=[feat, feat, feat, feat, posm, posm],
        compiler_params=pltpu.CompilerParams(
            dimension_semantics=("parallel", "parallel"), vmem_limit_bytes=VMEM_LIMIT),
        name="proj_rope",
    )(x, g1, wqv, wk, cqT, sqT, ck, sk)


def _diff_steps(q_ref, k_ref, v_ref, lq1_ref, lk1_ref, lq2_ref, lk2_ref, g_ref, o_ref, s_buf, p_buf,
                *, qb, nq):
    lam = (jnp.exp(jnp.sum(lq1_ref[...] * lk1_ref[...], axis=-1, keepdims=True))
           - jnp.exp(jnp.sum(lq2_ref[...] * lk2_ref[...], axis=-1, keepdims=True)) + LAM_INIT)
    gain = g_ref[...] * (1.0 - LAM_INIT)
    sub = qb // LANES
    row = lax.broadcasted_iota(jnp.int32, (PAIR, qb), 0)
    sub_head = (row % HEAD_DIM) // (DIFF_DIM // 2)
    kk = lax.broadcasted_iota(jnp.int32, (qb, qb), 0) // CHUNK
    qq = lax.broadcasted_iota(jnp.int32, (qb, qb), 1) // CHUNK
    diag_mask = jnp.where(kk <= qq, 0.0, NEG)
    ones = jnp.ones((ONES_ROWS, qb), BF16)

    NG = 4

    def make_w(i):
        qT = jnp.concatenate([q_ref[0, i * sub + t] for t in range(sub)], axis=1)
        zero = jnp.zeros_like(qT)
        return [jnp.where(sub_head == c, qT, zero) for c in range(NG)]

    def scores(w, blk, masked, slot, c):
        kblk = k_ref[0, 0, blk * qb:(blk + 1) * qb, :]
        s = _dot(kblk, w[c])
        if masked:
            s = s + diag_mask
        s_buf[slot, :, c * qb:(c + 1) * qb] = s
        return jnp.max(s, axis=0, keepdims=True)

    def weights(m_blk, m, slot, c):
        m_new = m_blk if m is None else jnp.maximum(m, m_blk)
        p_buf[slot, :, c * qb:(c + 1) * qb] = jnp.exp2(s_buf[slot, :, c * qb:(c + 1) * qb] - m_new).astype(BF16)
        return m_new, (None if m is None else jnp.exp2(m - m_new))

    def accum(blk, alpha, acc, slot, c):
        hh = c // 2
        vx = jnp.concatenate(
            [jnp.concatenate([v_ref[0, blk * sub + t, hh * HEAD_DIM:(hh + 1) * HEAD_DIM, :]
                              for t in range(sub)], axis=1), ones], axis=0)
        d = _dot(vx, p_buf[slot, :, c * qb:(c + 1) * qb])
        return d if acc is None else alpha * acc + d

    def finalize(i, accs):
        ys = []
        for hh in range(2):
            o = []
            for c in range(2):
                a = accs[2 * hh + c]
                o.append(a[:HEAD_DIM] * (1.0 / a[HEAD_DIM:HEAD_DIM + 1]))
            d = o[0] - lam * o[1]
            ms = jnp.mean(d * d, axis=0, keepdims=True)
            ys.append(d * lax.rsqrt(ms + EPS) * gain)
        y = jnp.concatenate(ys, axis=0)
        o_ref[0, 0, i * qb:(i + 1) * qb, :] = y.T.astype(BF16)

    ticks = [(i, blk) for i in range(nq) for blk in [i] + list(range(i))]
    last = {i: max(T for T, (ii, _) in enumerate(ticks) if ii == i) for i in range(nq)}
    st = {}

    def step(T):
        if T < len(ticks) and ticks[T][1] == ticks[T][0]:
            i = ticks[T][0]
            st[i] = dict(w=make_w(i), m=[None] * NG, alpha=[None] * NG, acc=[None] * NG, m_blk=[None] * NG)
        for c in range(NG):
            if T >= 2:
                i, blk = ticks[T - 2]
                S = st[i]
                S["acc"][c] = accum(blk, S["alpha"][c], S["acc"][c], T % 2, c)
            if 1 <= T <= len(ticks):
                i, blk = ticks[T - 1]
                S = st[i]
                S["m"][c], S["alpha"][c] = weights(S["m_blk"][c], S["m"][c], (T - 1) % 2, c)
            if T < len(ticks):
                i, blk = ticks[T]
                st[i]["m_blk"][c] = scores(st[i]["w"], blk, blk == i, T % 2, c)
        if T >= 2:
            i, blk = ticks[T - 2]
            if T - 2 == last[i]:
                finalize(i, st[i]["acc"])
                del st[i]

    return [functools.partial(step, T) for T in range(len(ticks) + 2)]


BAND_BLOCKS = LEFT_CHUNKS * CHUNK // LANES + 1
BAND_KEYS = BAND_BLOCKS * LANES


def _band_steps(q_ref, k_ref, v_ref, bias_ref, o_ref, s_buf, p_buf, *, nq, PP):
    row = lax.broadcasted_iota(jnp.int32, (PAIR, LANES), 0)
    rows_h = HEAD_DIM + ONES_ROWS

    GROUP = 2

    def span(i):
        nb = min(i + 1, BAND_BLOCKS)
        return i + 1 - nb, nb

    def groups(i):
        nb = span(i)[1]
        return [(g, min(g + GROUP, nb)) for g in range(0, nb, GROUP)]

    def make_w(i, pp):
        qT = q_ref[0, i, pp * PAIR:(pp + 1) * PAIR, :]
        zero = jnp.zeros_like(qT)
        return jnp.concatenate([jnp.where(row < HEAD_DIM, qT, zero),
                                jnp.where(row >= HEAD_DIM, qT, zero)], axis=1)

    def scores(i, pp, w, slot, g0, g1):
        lo, nb = span(i)
        kblk = k_ref[0, pp, (lo + g0) * LANES:(lo + g1) * LANES, :]
        s = _dot(kblk, w) + bias_ref[pp, (BAND_BLOCKS - nb + g0) * LANES:(BAND_BLOCKS - nb + g1) * LANES, :]
        s_buf[slot, pp, g0 * LANES:g1 * LANES, :] = s
        return jnp.max(s, axis=0, keepdims=True)

    def weights(pp, m, slot, g0, g1):
        p_buf[slot, pp, g0 * LANES:g1 * LANES, :] = jnp.exp2(
            s_buf[slot, pp, g0 * LANES:g1 * LANES, :] - m).astype(BF16)

    def partial_out(i, pp, slot, g0, g1):
        lo, nb = span(i)
        ones = jnp.ones((ONES_ROWS, (g1 - g0) * LANES), BF16)
        vx = jnp.concatenate(
            [part for hh in range(2) for part in (
                jnp.concatenate([v_ref[0, lo + t, pp * PAIR + hh * HEAD_DIM:pp * PAIR + (hh + 1) * HEAD_DIM, :]
                                 for t in range(g0, g1)], axis=1), ones)], axis=0)
        return _dot(vx, p_buf[slot, pp, g0 * LANES:g1 * LANES, :])

    def store(i, pp, out):
        ys = []
        for hh in range(2):
            blk = out[hh * rows_h:(hh + 1) * rows_h, hh * LANES:(hh + 1) * LANES]
            ys.append(blk[:HEAD_DIM] * (1.0 / blk[HEAD_DIM:HEAD_DIM + 1]))
        y = jnp.concatenate(ys, axis=0)
        o_ref[0, pp, i * LANES:(i + 1) * LANES, :] = y.T.astype(BF16)

    m = {}

    def step(i):
        w = [make_w(i, pp) for pp in range(PP)] if i < nq else None
        out, mg = [None] * PP, [None] * PP
        ng = max(len(groups(j)) for j in (i - 2, i - 1, i) if 0 <= j < nq)
        for n in range(ng):
            for pp in range(PP):
                if i >= 2 and n < len(groups(i - 2)):
                    d = partial_out(i - 2, pp, i % 2, *groups(i - 2)[n])
                    out[pp] = d if out[pp] is None else out[pp] + d
                if 1 <= i <= nq and n < len(groups(i - 1)):
                    weights(pp, m[i - 1][pp], (i - 1) % 2, *groups(i - 1)[n])
                if i < nq and n < len(groups(i)):
                    mm = scores(i, pp, w[pp], i % 2, *groups(i)[n])
                    mg[pp] = mm if mg[pp] is None else jnp.maximum(mg[pp], mm)
        for pp in range(PP):
            if i >= 2:
                store(i - 2, pp, out[pp])
        if i < nq:
            m[i] = mg

    return [functools.partial(step, i) for i in range(nq + 2)]


def _attn_kernel(qa_ref, ka_ref, va_ref, lq1_ref, lk1_ref, lq2_ref, lk2_ref, g_ref, qb_ref, kb_ref, vb_ref,
                 bias_ref, oa_ref, ob_ref, sa_buf, pa_buf, sb_buf, pb_buf, *, qb, nq_a, nq_b):
    a_steps = _diff_steps(qa_ref, ka_ref, va_ref, lq1_ref, lk1_ref, lq2_ref, lk2_ref, g_ref, oa_ref,
                          sa_buf, pa_buf, qb=qb, nq=nq_a)
    b_steps = _band_steps(qb_ref, kb_ref, vb_ref, bias_ref, ob_ref, sb_buf, pb_buf, nq=nq_b, PP=1)
    done = 0
    for n, a_step in enumerate(a_steps):
        a_step()
        while done < len(b_steps) and (done + 1) * len(a_steps) <= (n + 1) * len(b_steps):
            b_steps[done]()
            done += 1


def _attention(qaT, ka, vaT, lq1, lk1, lq2, lk2, g, qbT, kb, vbT, bias, *, qb):
    B, _, S, _ = ka.shape
    nblk = S // LANES
    feat_spec = pl.BlockSpec((1, nblk, PAIR, LANES), lambda b, p: (b, 0, p, 0))
    key_spec = pl.BlockSpec((1, 1, S, PAIR), lambda b, p: (b, p, 0, 0))
    vec = pl.BlockSpec((1, DIFF_DIM), lambda b, p: (0, 0))
    out = jax.ShapeDtypeStruct((B, N_PAIRS, S, PAIR), BF16)
    return pl.pallas_call(
        functools.partial(_attn_kernel, qb=qb, nq_a=S // qb, nq_b=nblk),
        grid=(B, N_PAIRS),
        in_specs=[feat_spec, key_spec, feat_spec, vec, vec, vec, vec,
                  pl.BlockSpec((HEAD_DIM, 1), lambda b, p: (0, 0)),
                  feat_spec, key_spec, feat_spec,
                  pl.BlockSpec((1, BAND_KEYS, 2 * LANES), lambda b, p: (p, 0, 0))],
        out_specs=[key_spec, key_spec],
        out_shape=[out, out],
        scratch_shapes=[pltpu.VMEM((2, qb, 4 * qb), F32), pltpu.VMEM((2, qb, 4 * qb), BF16),
                        pltpu.VMEM((2, 1, BAND_KEYS, 2 * LANES), F32),
                        pltpu.VMEM((2, 1, BAND_KEYS, 2 * LANES), BF16)],
        compiler_params=pltpu.CompilerParams(
            dimension_semantics=("parallel", "parallel"), vmem_limit_bytes=VMEM_LIMIT),
        name="attn",
    )(qaT, ka, vaT, lq1, lk1, lq2, lk2, g, qbT, kb, vbT, bias)


def _mlp_kernel(x_ref, oa_ref, ob_ref, woa_ref, wob_ref, g2_ref, w1_ref, w2_ref, gf_ref, out_ref,
                *, ff_chunk, sub):
    n_chunk = D_FF // ff_chunk
    pieces = x_ref.shape[0] // sub

    def stage_in(r):
        rows = slice(r * sub, (r + 1) * sub)
        oa = jnp.concatenate([oa_ref[0, g, rows, :] for g in range(N_PAIRS)], axis=1)
        ob = jnp.concatenate([ob_ref[0, g, rows, :] for g in range(N_PAIRS)], axis=1)
        return x_ref[rows, :] + _dot(oa, woa_ref[...]) + _dot(ob, wob_ref[...])

    def norm(y):
        ms = jnp.mean(y * y, axis=-1, keepdims=True)
        return ((y * lax.rsqrt(ms + EPS)) * g2_ref[...]).astype(BF16)

    def up(h, c):
        return _dot(h, w1_ref[:, c * ff_chunk:(c + 1) * ff_chunk])

    def down(u, c):
        u = jnp.maximum(u, 0.0)
        return _dot((u * u).astype(BF16), w2_ref[c * ff_chunk:(c + 1) * ff_chunk, :])

    def stage_out(r, y, mlp):
        acc = y + mlp
        ms = jnp.mean(acc * acc, axis=-1, keepdims=True)
        out_ref[r * sub:(r + 1) * sub, :] = (acc * lax.rsqrt(ms + EPS)) * gf_ref[...]

    st = [dict() for _ in range(pieces)]

    def event(r, e):
        S = st[r]
        if e == 0:
            S["y"] = stage_in(r)
        elif e == 1:
            S["h"] = norm(S["y"])
            S["u"] = up(S["h"], 0)
            S["mlp"] = None
        elif e < 2 + n_chunk:
            c = e - 2
            u_next = up(S["h"], c + 1) if c + 1 < n_chunk else None
            d = down(S["u"], c)
            S["mlp"] = d if S["mlp"] is None else S["mlp"] + d
            S["u"] = u_next
        else:
            stage_out(r, S["y"], S["mlp"])

    n_event = 3 + n_chunk
    for t in range(n_event + MLP_PIECE_SKEW * (pieces - 1)):
        for r in range(pieces):
            e = t - MLP_PIECE_SKEW * r
            if 0 <= e < n_event:
                event(r, e)


def _out_mlp(x2d, oa, ob, woa, wob, g2, w1, w2, gf, *, tm, ff_chunk):
    T = x2d.shape[0]
    tiles_per_seq = oa.shape[2] // tm
    mix_spec = pl.BlockSpec((1, N_PAIRS, tm, PAIR), lambda i: (i // tiles_per_seq, 0, i % tiles_per_seq, 0))
    const = lambda shape: pl.BlockSpec(shape, lambda i: (0,) * len(shape), pipeline_mode=pl.Buffered(1))
    return pl.pallas_call(
        functools.partial(_mlp_kernel, ff_chunk=ff_chunk, sub=MLP_PIECE_ROWS),
        grid=(T // tm,),
        in_specs=[
            pl.BlockSpec((tm, D_MODEL), lambda i: (i, 0)),
            mix_spec, mix_spec,
            const(woa.shape), const(wob.shape), const((1, D_MODEL)),
            const(w1.shape), const(w2.shape), const((1, D_MODEL)),
        ],
        out_specs=pl.BlockSpec((tm, D_MODEL), lambda i: (i, 0)),
        out_shape=jax.ShapeDtypeStruct((T, D_MODEL), F32),
        compiler_params=pltpu.CompilerParams(
            dimension_semantics=("parallel",), vmem_limit_bytes=VMEM_LIMIT),
        name="out_mlp",
    )(x2d, oa, ob, woa, wob, g2, w1, w2, gf)


def _pair_layout(wcols):
    half = DIFF_DIM // 2
    return (wcols.reshape(-1, N_PAIRS, PAIR // DIFF_DIM, 2, half).transpose(0, 1, 3, 2, 4)
            .reshape(-1, WIDTH_A))


def _rope_tables(S):
    half = DIFF_DIM // 2
    inv_freq = ROPE_THETA ** (-jnp.arange(half, dtype=F32) / half)
    ang = jnp.arange(S).astype(F32)[:, None] * inv_freq[None, :]
    cos, sin = jnp.cos(ang), jnp.sin(ang)
    scale = DIFF_DIM ** -0.5 * LOG2E
    cqT = jnp.tile(cos.T, (HEAD_DIM // half, 1)) * scale
    sqT = jnp.tile(sin.T, (HEAD_DIM // half, 1)) * scale
    ck = jnp.tile(cos, (1, PAIR // half))
    sign = jnp.where(jnp.arange(PAIR) < HEAD_DIM, -1.0, 1.0).astype(F32)
    sk = jnp.tile(sin, (1, PAIR // half)) * sign[None, :]
    return cqT, sqT, ck, sk


def _band_bias(rel_table):
    H = rel_table.shape[0]
    span, width = BAND_KEYS, 2 * BAND_KEYS
    back = LEFT_CHUNKS * CHUNK
    tbl = rel_table.astype(F32) * LOG2E
    left = span - MAX_REL_DIST
    right = width + 1 - left - tbl.shape[1]
    ext = jnp.concatenate([jnp.broadcast_to(tbl[:, :1], (H, left)), tbl,
                           jnp.broadcast_to(tbl[:, -1:], (H, right))], axis=1)
    wide = jnp.stack([ext[:, span - r:span - r + span] for r in range(LANES)], axis=1)
    vals = wide.reshape(H, LANES, BAND_BLOCKS, LANES)[:, :, ::-1, :].transpose(0, 2, 1, 3).reshape(H, span, LANES)
    j = jnp.arange(span)[:, None]
    q = back + jnp.arange(LANES)[None, :]
    dc = q // CHUNK - j // CHUNK
    ok = (dc >= 0) & (dc <= LEFT_CHUNKS)
    bias = jnp.where(ok[None], vals, NEG)
    return bias.reshape(N_PAIRS, 2, span, LANES).transpose(0, 2, 1, 3).reshape(N_PAIRS, span, 2 * LANES)


def kernel(x, w_in, w_out, norm1_g, norm2_g, final_g, subln_g, lambda_q1, lambda_k1, lambda_q2, lambda_k2,
           rel_bias, w_ff1, w_ff2):
    B, S, D = x.shape
    depth = w_in.shape[0]
    assert D == D_MODEL and depth == 1 and S % PROJ_ROWS == 0 and S % MLP_ROWS == 0
    l = 0
    w = w_in[l]
    c0, c1, c2, c3, c4 = np.cumsum([WIDTH_A, WIDTH_A, WIDTH_A, WIDTH_B, WIDTH_B]).tolist()
    wq_a = _pair_layout(w[:, :c0])
    wk_a = _pair_layout(w[:, c0:c1])
    wqv = jnp.concatenate([wq_a, w[:, c1:c2], w[:, c2:c3], w[:, c4:]], axis=1).T.astype(BF16)
    wk = jnp.concatenate([wk_a, w[:, c3:c4]], axis=1).astype(BF16)
    cqT, sqT, ck, sk = _rope_tables(S)

    qaT, vaT, qbT, vbT, ka, kb = _project(
        x, norm1_g[l].reshape(1, D), wqv, wk, cqT, sqT, ck, sk, tm=PROJ_ROWS)

    oa, ob = _attention(
        qaT, ka, vaT,
        lambda_q1[l].reshape(1, -1).astype(F32), lambda_k1[l].reshape(1, -1).astype(F32),
        lambda_q2[l].reshape(1, -1).astype(F32), lambda_k2[l].reshape(1, -1).astype(F32),
        subln_g[l].reshape(HEAD_DIM, 1), qbT, kb, vbT, _band_bias(rel_bias[l]), qb=DIFF_BLOCK)

    wo = w_out[l].astype(BF16)
    out = _out_mlp(
        x.reshape(B * S, D), oa, ob,
        wo[:WIDTH_A], wo[WIDTH_A:], norm2_g[l].reshape(1, D),
        w_ff1[l].astype(BF16), w_ff2[l].astype(BF16), final_g.reshape(1, D),
        tm=MLP_ROWS, ff_chunk=MLP_FF_CHUNK)
    return out.reshape(B, S, D)
```
